```python
import math
import jax, jax.numpy as jnp
from jax import lax
import numpy as np

D_MODEL = 2048
BATCH = 4
SEQ = 2048
DEPTH = 4

GRID_W = 64
CTX_LEN = 256
HEAD_DIM = 128
BLOCK = 128
WINDOW = 128
A_HEADS = (D_MODEL // 2) // HEAD_DIM
A_KV_HEADS = max(1, A_HEADS // 4)
A_GROUP = A_HEADS // A_KV_HEADS
A_WIDTH = A_HEADS * HEAD_DIM
B_VDIM = 2 * HEAD_DIM
B_HEADS = (D_MODEL // 2) // B_VDIM
B_WIDTH = B_HEADS * B_VDIM
MIX_WIDTH = A_WIDTH + B_WIDTH
A_Q = A_HEADS * HEAD_DIM
A_KV = A_KV_HEADS * HEAD_DIM
B_QK = B_HEADS * 2 * HEAD_DIM
B_V = B_HEADS * B_VDIM
IN_COLS = A_Q + 2 * A_KV + 2 * B_QK + B_V
IN_SPLITS = (A_Q, A_Q + A_KV, A_Q + 2 * A_KV, A_Q + 2 * A_KV + B_QK, A_Q + 2 * A_KV + 2 * B_QK)
D_FF = (11 * D_MODEL) // 4
N_EXPERTS = 8
TOP_K = 2
N_DENSE = (DEPTH + 1) // 2
N_MOE = DEPTH // 2
N_MOD = 6
ROPE_BASE = 10000.0
ROPE_PAIRS = HEAD_DIM // 4
ATTN_SCALE = HEAD_DIM ** -0.5
EPS = 1e-6
NEG = -1e30

kernel_name = "hybrid_dit_swa_diffattn_moe"


def rmsnorm(x, g):
    xf = x.astype(jnp.float32)
    y = xf * lax.rsqrt(jnp.mean(xf * xf, axis=-1, keepdims=True) + EPS)
    return (y * g.astype(jnp.float32)).astype(x.dtype)


def modulate(h, shift, scale):
    return h * (1.0 + scale) + shift


def axial_rope_tables(n_tokens):
    rows = n_tokens // GRID_W
    row = jnp.repeat(jnp.arange(rows), GRID_W).astype(jnp.float32)
    col = jnp.tile(jnp.arange(GRID_W), rows).astype(jnp.float32)
    inv = ROPE_BASE ** (-jnp.arange(ROPE_PAIRS, dtype=jnp.float32) / ROPE_PAIRS)
    ang_r = row[:, None] * inv[None, :]
    ang_c = col[:, None] * inv[None, :]
    return (jnp.cos(ang_r), jnp.sin(ang_r), jnp.cos(ang_c), jnp.sin(ang_c))


def apply_axial_rope(x, tabs):
    cr, sr, cc, scl = [t.reshape((t.shape[0],) + (1,) * (x.ndim - 3) + (t.shape[1],)).astype(x.dtype) for t in tabs]
    a1, a2, b1, b2 = jnp.split(x, 4, axis=-1)
    return jnp.concatenate([a1 * cr - a2 * sr, a2 * cr + a1 * sr,
                            b1 * cc - b2 * scl, b2 * cc + b1 * scl], axis=-1)


def split_in_proj(p):
    bsz, n = p.shape[:2]
    qa, ka, va, qb, kb, vb = jnp.split(p, IN_SPLITS, axis=-1)
    return (qa.reshape(bsz, n, A_KV_HEADS, A_GROUP, HEAD_DIM),
            ka.reshape(bsz, n, A_KV_HEADS, HEAD_DIM),
            va.reshape(bsz, n, A_KV_HEADS, HEAD_DIM),
            qb.reshape(bsz, n, B_HEADS, 2, HEAD_DIM),
            kb.reshape(bsz, n, B_HEADS, 2, HEAD_DIM),
            vb.reshape(bsz, n, B_HEADS, B_VDIM))


def window_sink_attention(q, k, v, kc, vc, sink):
    bsz, n = q.shape[:2]
    nb = n // BLOCK
    qb = q.reshape(bsz, nb, BLOCK, A_KV_HEADS, A_GROUP, HEAD_DIM)

    def band(t):
        tp = jnp.pad(t, ((0, 0), (BLOCK, BLOCK), (0, 0), (0, 0))).reshape(bsz, nb + 2, BLOCK, A_KV_HEADS, HEAD_DIM)
        return jnp.concatenate([tp[:, :-2], tp[:, 1:-1], tp[:, 2:]], axis=2)

    kb, vb = band(k), band(v)
    qpos = jnp.arange(n).reshape(nb, BLOCK)
    kpos = (jnp.arange(nb)[:, None] - 1) * BLOCK + jnp.arange(3 * BLOCK)[None, :]
    valid = (kpos[:, None, :] >= 0) & (kpos[:, None, :] < n) & (jnp.abs(qpos[:, :, None] - kpos[:, None, :]) <= WINDOW)
    s_loc = jnp.einsum('bnqkgd,bnjkd->bnkgqj', qb, kb).astype(jnp.float32) * ATTN_SCALE
    s_loc = jnp.where(valid[None, :, None, None], s_loc, NEG)
    s_ctx = jnp.einsum('bnqkgd,bckd->bnkgqc', qb, kc).astype(jnp.float32) * ATTN_SCALE
    sink_col = jnp.broadcast_to(sink.astype(jnp.float32)[None, None, :, :, None, None], s_loc.shape[:-1] + (1,))
    p = jax.nn.softmax(jnp.concatenate([s_loc, s_ctx, sink_col], axis=-1), axis=-1).astype(v.dtype)
    nl = 3 * BLOCK
    out = (jnp.einsum('bnkgqj,bnjkd->bnqkgd', p[..., :nl], vb)
           + jnp.einsum('bnkgqc,bckd->bnqkgd', p[..., nl:nl + kc.shape[1]], vc))
    return out.reshape(bsz, n, A_WIDTH)


def context_sink_attention(qc, kc, vc, sink):
    bsz, n = qc.shape[:2]
    s = jnp.einsum('bqkgd,bckd->bkgqc', qc, kc).astype(jnp.float32) * ATTN_SCALE
    sink_col = jnp.broadcast_to(sink.astype(jnp.float32)[None, :, :, None, None], s.shape[:-1] + (1,))
    p = jax.nn.softmax(jnp.concatenate([s, sink_col], axis=-1), axis=-1)[..., :-1].astype(vc.dtype)
    out = jnp.einsum('bkgqc,bckd->bqkgd', p, vc)
    return out.reshape(bsz, n, A_WIDTH)


def differential_attention(q, k_all, v_all, lam):
    bsz, n = q.shape[:2]
    nb = n // BLOCK
    qb = jnp.moveaxis(q.reshape(bsz, nb, BLOCK, B_HEADS, 2, HEAD_DIM), 1, 0)

    def one_block(qi):
        s = jnp.einsum('bqhmd,bthmd->bhmqt', qi, k_all).astype(jnp.float32) * ATTN_SCALE
        p = jax.nn.softmax(s, axis=-1)
        a = (p[:, :, 0] - lam * p[:, :, 1]).astype(v_all.dtype)
        return jnp.einsum('bhqt,bthe->bqhe', a, v_all)

    out = lax.map(one_block, qb)
    return jnp.moveaxis(out, 0, 1).reshape(bsz, n, B_HEADS, B_VDIM)


def diff_head_norm(o, g, lam_init):
    bsz, n = o.shape[:2]
    return (rmsnorm(o, g) * (1.0 - lam_init)).reshape(bsz, n, B_WIDTH)


def swiglu(h, wg, wu, wd):
    return (jax.nn.silu(h @ wg) * (h @ wu)) @ wd


def moe_swiglu(h, w_router, wg, wu, wd):
    shp = h.shape
    t = h.reshape(-1, shp[-1])
    logits = (t @ w_router).astype(jnp.float32)
    top_v, top_i = lax.top_k(logits, TOP_K)
    wts = jax.nn.softmax(top_v, axis=-1)
    combine = jnp.sum(jax.nn.one_hot(top_i, N_EXPERTS, dtype=jnp.float32) * wts[..., None], axis=1)
    out = jnp.zeros_like(t)
    for e in range(N_EXPERTS):
        out = out + combine[:, e:e + 1].astype(t.dtype) * swiglu(t, wg[e], wu[e], wd[e])
    return out.reshape(shp)


def setup_inputs(seed: int = 0) -> dict:
    key = jax.random.key(seed)
    ks = jax.random.split(key, 26)
    d = D_MODEL

    def nrm(k, shape, scale):
        return jax.random.normal(k, shape, jnp.float32) * scale

    return {
        "x": nrm(ks[0], (BATCH, SEQ, d), 1.0),
        "c": nrm(ks[1], (BATCH, d), 1.0),
        "ctx": nrm(ks[2], (BATCH, CTX_LEN, d), 1.0),
        "c_ctx": nrm(ks[3], (d,), 1.0),
        "w_mod": nrm(ks[4], (DEPTH, d, N_MOD * d), 0.5 * d ** -0.5),
        "b_mod": nrm(ks[5], (DEPTH, N_MOD * d), 0.02),
        "g_attn_pre": 1.0 + nrm(ks[6], (DEPTH, d), 0.05),
        "g_attn_post": 1.0 + nrm(ks[7], (DEPTH, d), 0.05),
        "g_ffn_pre": 1.0 + nrm(ks[8], (DEPTH, d), 0.05),
        "g_ffn_post": 1.0 + nrm(ks[9], (DEPTH, d), 0.05),
        "w_in": nrm(ks[10], (DEPTH, d, IN_COLS), d ** -0.5),
        "w_out": nrm(ks[11], (DEPTH, MIX_WIDTH, d), MIX_WIDTH ** -0.5),
        "sink_logit": nrm(ks[12], (DEPTH, A_KV_HEADS, A_GROUP), 0.5),
        "lambda_q1": nrm(ks[13], (DEPTH, HEAD_DIM), 0.1),
        "lambda_k1": nrm(ks[14], (DEPTH, HEAD_DIM), 0.1),
        "lambda_q2": nrm(ks[15], (DEPTH, HEAD_DIM), 0.1),
        "lambda_k2": nrm(ks[16], (DEPTH, HEAD_DIM), 0.1),
        "subln_g": 1.0 + nrm(ks[17], (DEPTH, B_VDIM), 0.05),
        "ffn_w_gate": nrm(ks[18], (N_DENSE, d, D_FF), d ** -0.5),
        "ffn_w_up": nrm(ks[19], (N_DENSE, d, D_FF), d ** -0.5),
        "ffn_w_down": nrm(ks[20], (N_DENSE, D_FF, d), D_FF ** -0.5),
        "moe_router": nrm(ks[21], (N_MOE, d, N_EXPERTS), d ** -0.5),
        "moe_w_gate": nrm(ks[22], (N_MOE, N_EXPERTS, d, D_FF), d ** -0.5),
        "moe_w_up": nrm(ks[23], (N_MOE, N_EXPERTS, d, D_FF), d ** -0.5),
        "moe_w_down": nrm(ks[24], (N_MOE, N_EXPERTS, D_FF, d), D_FF ** -0.5),
    }


def reference(x, c, ctx, c_ctx, w_mod, b_mod, g_attn_pre, g_attn_post, g_ffn_pre, g_ffn_post,
              w_in, w_out, sink_logit, lambda_q1, lambda_k1, lambda_q2, lambda_k2, subln_g,
              ffn_w_gate, ffn_w_up, ffn_w_down, moe_router, moe_w_gate, moe_w_up, moe_w_down):
    n = x.shape[1]
    tabs = axial_rope_tables(n)
    lat, cst = x, ctx
    c_act = jax.nn.silu(c)
    cc_act = jax.nn.silu(c_ctx)
    for l in range(DEPTH):
        last = l == DEPTH - 1
        mod_l = (c_act @ w_mod[l] + b_mod[l])[:, None, :]
        mod_c = cc_act @ w_mod[l] + b_mod[l]
        sh_a, sc_a, gt_a, sh_f, sc_f, gt_f = jnp.split(mod_l, N_MOD, axis=-1)
        csh_a, csc_a, cgt_a, csh_f, csc_f, cgt_f = jnp.split(mod_c, N_MOD, axis=-1)

        h_lat = modulate(rmsnorm(lat, g_attn_pre[l]), sh_a, sc_a)
        h_ctx = modulate(rmsnorm(cst, g_attn_pre[l]), csh_a, csc_a)
        qa, ka, va, qb, kb, vb = split_in_proj(h_lat @ w_in[l])
        qa_c, ka_c, va_c, qb_c, kb_c, vb_c = split_in_proj(h_ctx @ w_in[l])
        qa, ka = apply_axial_rope(qa, tabs), apply_axial_rope(ka, tabs)
        qb, kb = apply_axial_rope(qb, tabs), apply_axial_rope(kb, tabs)

        lam_init = 0.8 - 0.6 * math.exp(-0.3 * l)
        lam = (jnp.exp(jnp.sum(lambda_q1[l].astype(jnp.float32) * lambda_k1[l].astype(jnp.float32)))
               - jnp.exp(jnp.sum(lambda_q2[l].astype(jnp.float32) * lambda_k2[l].astype(jnp.float32)))
               + lam_init)

        oa = window_sink_attention(qa, ka, va, ka_c, va_c, sink_logit[l])
        k_all = jnp.concatenate([kb_c, kb], axis=1)
        v_all = jnp.concatenate([vb_c, vb], axis=1)
        ob = diff_head_norm(differential_attention(qb, k_all, v_all, lam), subln_g[l], lam_init)
        y = jnp.concatenate([oa, ob], axis=-1) @ w_out[l]
        lat_new = lat + gt_a * rmsnorm(y, g_attn_post[l])
        if not last:
            oa_c = context_sink_attention(qa_c, ka_c, va_c, sink_logit[l])
            ob_c = diff_head_norm(differential_attention(qb_c, kb_c, vb_c, lam), subln_g[l], lam_init)
            y_c = jnp.concatenate([oa_c, ob_c], axis=-1) @ w_out[l]
            cst = cst + cgt_a * rmsnorm(y_c, g_attn_post[l])
        lat = lat_new

        if l % 2 == 0:
            i = l // 2
            def ffn(h, i=i):
                return swiglu(h, ffn_w_gate[i], ffn_w_up[i], ffn_w_down[i])
        else:
            i = l // 2
            def ffn(h, i=i):
                return moe_swiglu(h, moe_router[i], moe_w_gate[i], moe_w_up[i], moe_w_down[i])
        f_lat = ffn(modulate(rmsnorm(lat, g_ffn_pre[l]), sh_f, sc_f))
        lat = lat + gt_f * rmsnorm(f_lat, g_ffn_post[l])
        if not last:
            f_ctx = ffn(modulate(rmsnorm(cst, g_ffn_pre[l]), csh_f, csc_f))
            cst = cst + cgt_f * rmsnorm(f_ctx, g_ffn_post[l])
    return lat
```

```python
import functools
import math

import jax
import jax.numpy as jnp
from jax import lax
from jax.experimental import pallas as pl
from jax.experimental.pallas import tpu as pltpu

D = 2048
B = 4
SEQ = 2048
DEPTH = 4
GRID_W = 64
CTX = 256
HD = 128
WINDOW = 128
A_HEADS = 8
A_KVH = 2
A_GROUP = 4
B_HEADS = 4
B_VDIM = 256
FF = 5632
NE = 8
N_MOD = 6
ROPE_BASE = 10000.0
ROPE_PAIRS = 32
ATTN_SCALE = HD ** -0.5
EPS = 1e-6
NEG = -1e30

NL = B * SEQ
NC = B * CTX
NR = NL + NC
IN_COLS = 4608
COL_QA, COL_QB, COL_KB, COL_VB, COL_KA, COL_VA = 0, 1024, 2048, 3072, 4096, 4352

LANES = 128
VMEM_LIMIT = 56 * 1024 * 1024

F32 = jnp.float32
BF16 = jnp.bfloat16
HIGHEST = lax.Precision.HIGHEST


def _cparams(sem):
    return pltpu.CompilerParams(dimension_semantics=sem, vmem_limit_bytes=VMEM_LIMIT)


def _nt_dot(a, b):
    return lax.dot_general(a, b, (((1,), (1,)), ((), ())), preferred_element_type=F32)


def _rms(x):
    return x * lax.rsqrt(jnp.mean(x * x, axis=-1, keepdims=True) + EPS)


def _norm_mod(x, g, sh, sc):
    return (_rms(x) * g) * (1.0 + sc) + sh


def _mod_row(i, tm):
    return jnp.where(i < NL // tm, (i * tm) // SEQ, B)


def _mod_spec(tm, chunk, grid_rank=1):
    if grid_rank == 1:
        return pl.BlockSpec((1, 1, D), lambda i: (_mod_row(i, tm), 0, chunk))
    return pl.BlockSpec((1, 1, D), lambda i, j: (_mod_row(i, tm), 0, chunk))


MOD_TN = 1024


def _mod_kernel(c_ref, w_ref, b_ref, o_ref):
    c = c_ref[...]
    a = c * jax.nn.sigmoid(c)
    o_ref[0] = jnp.dot(a, w_ref[0], precision=HIGHEST, preferred_element_type=F32) + b_ref[0]


def _modulation(cin, w_mod, b_mod):
    return pl.pallas_call(
        _mod_kernel,
        out_shape=jax.ShapeDtypeStruct((DEPTH, 8, N_MOD * D), F32),
        grid=(DEPTH, N_MOD * D // MOD_TN),
        in_specs=[
            pl.BlockSpec((8, D), lambda l, n: (0, 0)),
            pl.BlockSpec((1, D, MOD_TN), lambda l, n: (l, 0, n)),
            pl.BlockSpec((1, 1, MOD_TN), lambda l, n: (l, 0, n)),
        ],
        out_specs=pl.BlockSpec((1, 8, MOD_TN), lambda l, n: (l, 0, n)),
        compiler_params=_cparams(("arbitrary", "arbitrary")),
        name="modulation",
    )(cin, w_mod, b_mod.reshape(DEPTH, 1, N_MOD * D))


IN_TM = 256
IN_CH = 512


def _inproj_kernel(x_ref, g_ref, sh_ref, sc_ref, cos_ref, sin_ref, w_ref, o_ref):
    hb = _norm_mod(x_ref[...], g_ref[...], sh_ref[0], sc_ref[0]).astype(BF16)
    cos = cos_ref[...]
    sin = sin_ref[...]
    lane = lax.broadcasted_iota(jnp.int32, (IN_TM, LANES), 1)
    first = (lane & 63) < 32
    for c in range(IN_COLS // IN_CH):
        acc = jnp.dot(hb, w_ref[:, c * IN_CH:(c + 1) * IN_CH], preferred_element_type=F32)
        for s in range(IN_CH // LANES):
            col = c * IN_CH + s * LANES
            y = acc[:, s * LANES:(s + 1) * LANES]
            if col < COL_VB or COL_KA <= col < COL_VA:
                partner = jnp.where(first, pltpu.roll(y, 96, 1), pltpu.roll(y, 32, 1))
                y = y * cos + partner * sin
                if col < COL_KB:
                    y = y * ATTN_SCALE
            o_ref[:, col:col + LANES] = y.astype(BF16)


def _in_proj(x, g, mod, w_bf, cos_t, sin_t):
    rows = x.shape[0]
    tm = IN_TM
    tab = lambda i: (jnp.where(i < NL // tm, i % (SEQ // tm), SEQ // tm), 0)
    return pl.pallas_call(
        _inproj_kernel,
        out_shape=jax.ShapeDtypeStruct((rows, IN_COLS), BF16),
        grid=(rows // tm,),
        in_specs=[
            pl.BlockSpec((tm, D), lambda i: (i, 0)),
            pl.BlockSpec((1, D), lambda i: (0, 0)),
            _mod_spec(tm, 0),
            _mod_spec(tm, 1),
            pl.BlockSpec((tm, LANES), tab),
            pl.BlockSpec((tm, LANES), tab),
            pl.BlockSpec((D, IN_COLS), lambda i: (0, 0)),
        ],
        out_specs=pl.BlockSpec((tm, IN_COLS), lambda i: (i, 0)),
        compiler_params=_cparams(("arbitrary",)),
        name="in_proj",
    )(x, g.reshape(1, D), mod, mod, cos_t, sin_t, w_bf)


ATT_TQ = 256
A_WIN = ATT_TQ + 2 * WINDOW


def _attn_a_kernel(sink_ref, q_ref, kc_ref, vc_ref, *rest, local):
    if local:
        kl_ref, vl_ref, o_ref = rest
    else:
        (o_ref,) = rest
    tq = ATT_TQ
    i = pl.program_id(1)
    for kv in range(A_KVH):
        ksl = slice(kv * HD, (kv + 1) * HD)
        qs = jnp.concatenate(
            [q_ref[:, (kv * A_GROUP + g) * HD:(kv * A_GROUP + g + 1) * HD] for g in range(A_GROUP)], axis=0)
        sink = jnp.concatenate(
            [jnp.full((tq, 1), sink_ref[kv * A_GROUP + g], F32) for g in range(A_GROUP)], axis=0)
        s_ctx = _nt_dot(qs, kc_ref[:, ksl])
        m = jnp.maximum(jnp.max(s_ctx, axis=-1, keepdims=True), sink)
        if local:
            st = pl.multiple_of(jnp.clip(i * tq - WINDOW, 0, SEQ - A_WIN), WINDOW)
            s_loc = _nt_dot(qs, kl_ref[pl.ds(st, A_WIN), ksl])
            qpos = i * tq + (lax.broadcasted_iota(jnp.int32, (A_GROUP * tq, A_WIN), 0) & (tq - 1))
            kpos = st + lax.broadcasted_iota(jnp.int32, (A_GROUP * tq, A_WIN), 1)
            s_loc = jnp.where(jnp.abs(qpos - kpos) <= WINDOW, s_loc, NEG)
            m = jnp.maximum(m, jnp.max(s_loc, axis=-1, keepdims=True))
        p_ctx = jnp.exp(s_ctx - m)
        den = jnp.sum(p_ctx, axis=-1, keepdims=True) + jnp.exp(sink - m)
        o = jnp.dot(p_ctx.astype(BF16), vc_ref[:, ksl], preferred_element_type=F32)
        if local:
            p_loc = jnp.exp(s_loc - m)
            den = den + jnp.sum(p_loc, axis=-1, keepdims=True)
            o = o + jnp.dot(p_loc.astype(BF16), vl_ref[pl.ds(st, A_WIN), ksl], preferred_element_type=F32)
        o = o * (1.0 / den)
        for g in range(A_GROUP):
            h = kv * A_GROUP + g
            o_ref[:, h * HD:(h + 1) * HD] = o[g * tq:(g + 1) * tq].astype(BF16)


def _attn_a(qkv, sink, local):
    tq = ATT_TQ
    nq = SEQ // tq if local else CTX // tq
    row0 = 0 if local else NL // tq
    per_b = nq
    ctx0 = NL // CTX
    in_specs = [
        pl.BlockSpec(memory_space=pltpu.SMEM),
        pl.BlockSpec((tq, 1024), lambda b, i: (row0 + b * per_b + i, COL_QA // 1024)),
        pl.BlockSpec((CTX, 256), lambda b, i: (ctx0 + b, COL_KA // 256)),
        pl.BlockSpec((CTX, 256), lambda b, i: (ctx0 + b, COL_VA // 256)),
    ]
    args = [sink.reshape(A_HEADS), qkv, qkv, qkv]
    if local:
        in_specs += [
            pl.BlockSpec((SEQ, 256), lambda b, i: (b, COL_KA // 256)),
            pl.BlockSpec((SEQ, 256), lambda b, i: (b, COL_VA // 256)),
        ]
        args += [qkv, qkv]
    rows = NL if local else NC
    return pl.pallas_call(
        functools.partial(_attn_a_kernel, local=local),
        out_shape=jax.ShapeDtypeStruct((rows, 1024), BF16),
        grid=(B, nq),
        in_specs=in_specs,
        out_specs=pl.BlockSpec((tq, 1024), lambda b, i: (b * per_b + i, 0)),
        compiler_params=_cparams(("arbitrary", "arbitrary")),
        name="attn_a_lat" if local else "attn_a_ctx",
    )(*args)


def _attn_b_kernel(lam_ref, g_ref, q_ref, kc_ref, vc_ref, *rest, with_lat, lam_init):
    if with_lat:
        kl_ref, vl_ref, o_ref = rest
    else:
        (o_ref,) = rest
    lv = lam_ref[...]
    lam = (jnp.exp(jnp.sum(lv[0:1] * lv[1:2], axis=-1, keepdims=True))
           - jnp.exp(jnp.sum(lv[2:3] * lv[3:4], axis=-1, keepdims=True)) + lam_init)
    for h in range(B_HEADS):
        pc, plat = [], []
        for mth in range(2):
            sl = slice((2 * h + mth) * HD, (2 * h + mth + 1) * HD)
            q = q_ref[:, sl]
            s_c = _nt_dot(q, kc_ref[:, sl])
            mx = jnp.max(s_c, axis=-1, keepdims=True)
            if with_lat:
                s_l = _nt_dot(q, kl_ref[:, sl])
                mx = jnp.maximum(mx, jnp.max(s_l, axis=-1, keepdims=True))
            e_c = jnp.exp(s_c - mx)
            den = jnp.sum(e_c, axis=-1, keepdims=True)
            if with_lat:
                e_l = jnp.exp(s_l - mx)
                den = den + jnp.sum(e_l, axis=-1, keepdims=True)
            r = 1.0 / den
            pc.append(e_c * r)
            if with_lat:
                plat.append(e_l * r)
        vsl = slice(h * B_VDIM, (h + 1) * B_VDIM)
        o = jnp.dot((pc[0] - lam * pc[1]).astype(BF16), vc_ref[:, vsl], preferred_element_type=F32)
        if with_lat:
            o = o + jnp.dot((plat[0] - lam * plat[1]).astype(BF16), vl_ref[:, vsl],
                            preferred_element_type=F32)
        o = (_rms(o) * g_ref[...]) * (1.0 - lam_init)
        o_ref[:, vsl] = o.astype(BF16)


def _attn_b(qkv, lam_vecs, subln_g, lam_init, with_lat):
    tq = ATT_TQ
    nq = SEQ // tq if with_lat else CTX // tq
    row0 = 0 if with_lat else NL // tq
    ctx0 = NL // CTX
    in_specs = [
        pl.BlockSpec((4, HD), lambda b, i: (0, 0)),
        pl.BlockSpec((1, B_VDIM), lambda b, i: (0, 0)),
        pl.BlockSpec((tq, 1024), lambda b, i: (row0 + b * nq + i, COL_QB // 1024)),
        pl.BlockSpec((CTX, 1024), lambda b, i: (ctx0 + b, COL_KB // 1024)),
        pl.BlockSpec((CTX, 1024), lambda b, i: (ctx0 + b, COL_VB // 1024)),
    ]
    args = [lam_vecs, subln_g.reshape(1, B_VDIM), qkv, qkv, qkv]
    if with_lat:
        in_specs += [
            pl.BlockSpec((SEQ, 1024), lambda b, i: (b, COL_KB // 1024)),
            pl.BlockSpec((SEQ, 1024), lambda b, i: (b, COL_VB // 1024)),
        ]
        args += [qkv, qkv]
    rows = NL if with_lat else NC
    return pl.pallas_call(
        functools.partial(_attn_b_kernel, with_lat=with_lat, lam_init=lam_init),
        out_shape=jax.ShapeDtypeStruct((rows, 1024), BF16),
        grid=(B, nq),
        in_specs=in_specs,
        out_specs=pl.BlockSpec((tq, 1024), lambda b, i: (b * nq + i, 0)),
        compiler_params=_cparams(("arbitrary", "arbitrary")),
        name="attn_b_lat" if with_lat else "attn_b_ctx",
    )(*args)


OUT_TM = 512


def _outproj_kernel(x_ref, oa_ref, ob_ref, w_ref, g_ref, gate_ref, o_ref):
    y = (jnp.dot(oa_ref[...], w_ref[0:1024, :], preferred_element_type=F32)
         + jnp.dot(ob_ref[...], w_ref[1024:2048, :], preferred_element_type=F32))
    o_ref[...] = x_ref[...] + gate_ref[0] * (_rms(y) * g_ref[...])


def _out_proj(x, oa, ob, w_bf, g, mod, rows):
    tm = OUT_TM
    return pl.pallas_call(
        _outproj_kernel,
        out_shape=jax.ShapeDtypeStruct((rows, D), F32),
        grid=(rows // tm,),
        in_specs=[
            pl.BlockSpec((tm, D), lambda i: (i, 0)),
            pl.BlockSpec((tm, 1024), lambda i: (i, 0)),
            pl.BlockSpec((tm, 1024), lambda i: (i, 0)),
            pl.BlockSpec((D, D), lambda i: (0, 0)),
            pl.BlockSpec((1, D), lambda i: (0, 0)),
            _mod_spec(tm, 2),
        ],
        out_specs=pl.BlockSpec((tm, D), lambda i: (i, 0)),
        compiler_params=_cparams(("arbitrary",)),
        name="out_proj",
    )(x, oa, ob, w_bf, g.reshape(1, D), mod)


FFN_TM = 512
FFN_TF = 512


def _swiglu_step(hb, wg, wu, wd):
    gt = jnp.dot(hb, wg, preferred_element_type=F32)
    up = jnp.dot(hb, wu, preferred_element_type=F32)
    act = (gt * jax.nn.sigmoid(gt) * up).astype(BF16)
    return jnp.dot(act, wd, preferred_element_type=F32)


def _ffn_kernel(x_ref, gpre_ref, sh_ref, sc_ref, gate_ref, gpost_ref, wg_ref, wu_ref, wd_ref, o_ref,
                h_scr, acc_scr):
    j = pl.program_id(1)

    @pl.when(j == 0)
    def _():
        h_scr[...] = _norm_mod(x_ref[...], gpre_ref[...], sh_ref[0], sc_ref[0]).astype(BF16)

    contrib = _swiglu_step(h_scr[...], wg_ref[...], wu_ref[...], wd_ref[...])

    @pl.when(j == 0)
    def _():
        acc_scr[...] = contrib

    @pl.when(j > 0)
    def _():
        acc_scr[...] += contrib

    @pl.when(j == pl.num_programs(1) - 1)
    def _():
        o_ref[...] = x_ref[...] + gate_ref[0] * (_rms(acc_scr[...]) * gpost_ref[...])


def _ffn_dense(x, gpre, gpost, mod, wg, wu, wd, rows):
    tm, tf = FFN_TM, FFN_TF
    return pl.pallas_call(
        _ffn_kernel,
        out_shape=jax.ShapeDtypeStruct((rows, D), F32),
        grid=(rows // tm, FF // tf),
        in_specs=[
            pl.BlockSpec((tm, D), lambda i, j: (i, 0)),
            pl.BlockSpec((1, D), lambda i, j: (0, 0)),
            _mod_spec(tm, 3, 2),
            _mod_spec(tm, 4, 2),
            _mod_spec(tm, 5, 2),
            pl.BlockSpec((1, D), lambda i, j: (0, 0)),
            pl.BlockSpec((D, tf), lambda i, j: (0, j)),
            pl.BlockSpec((D, tf), lambda i, j: (0, j)),
            pl.BlockSpec((tf, D), lambda i, j: (j, 0)),
        ],
        out_specs=pl.BlockSpec((tm, D), lambda i, j: (i, 0)),
        scratch_shapes=[pltpu.VMEM((tm, D), BF16), pltpu.VMEM((tm, D), F32)],
        compiler_params=_cparams(("arbitrary", "arbitrary")),
        name="ffn_dense",
    )(x, gpre.reshape(1, D), mod, mod, mod, gpost.reshape(1, D), wg, wu, wd)


RT_TM = 512
MOE_TM = 512
MOE_TF = 512


def _router_kernel(x_ref, gpre_ref, sh_ref, sc_ref, wr_ref, h_ref, ri_ref, rw_ref):
    h = _norm_mod(x_ref[...], gpre_ref[...], sh_ref[0], sc_ref[0])
    h_ref[...] = h
    logits = jnp.dot(h, wr_ref[...], precision=HIGHEST, preferred_element_type=F32)
    lane = lax.broadcasted_iota(jnp.int32, logits.shape, 1)
    logits = jnp.where(lane < NE, logits, -jnp.inf)
    m1 = jnp.max(logits, axis=-1, keepdims=True)
    i1 = jnp.min(jnp.where(logits == m1, lane, LANES), axis=-1, keepdims=True)
    rest = jnp.where(lane == i1, -jnp.inf, logits)
    m2 = jnp.max(rest, axis=-1, keepdims=True)
    i2 = jnp.min(jnp.where(rest == m2, lane, LANES), axis=-1, keepdims=True)
    e2 = jnp.exp(m2 - m1)
    den = 1.0 + e2
    ri_ref[...] = jnp.where(lane == 0, i1, jnp.where(lane == 1, i2, 0))
    rw_ref[...] = jnp.where(lane == 0, 1.0 / den, jnp.where(lane == 1, e2 / den, 0.0))


def _router(x, gpre, mod, wr_pad, rows):
    tm = RT_TM
    return pl.pallas_call(
        _router_kernel,
        out_shape=(jax.ShapeDtypeStruct((rows, D), F32),
                   jax.ShapeDtypeStruct((rows, LANES), jnp.int32),
                   jax.ShapeDtypeStruct((rows, LANES), F32)),
        grid=(rows // tm,),
        in_specs=[
            pl.BlockSpec((tm, D), lambda i: (i, 0)),
            pl.BlockSpec((1, D), lambda i: (0, 0)),
            _mod_spec(tm, 3),
            _mod_spec(tm, 4),
            pl.BlockSpec((D, LANES), lambda i: (0, 0)),
        ],
        out_specs=(pl.BlockSpec((tm, D), lambda i: (i, 0)),
                   pl.BlockSpec((tm, LANES), lambda i: (i, 0)),
                   pl.BlockSpec((tm, LANES), lambda i: (i, 0))),
        compiler_params=_cparams(("arbitrary",)),
        name="moe_router",
    )(x, gpre.reshape(1, D), mod, mod, wr_pad)


def _moe_kernel(te_ref, tv_ref, src_ref, dst_ref, h_hbm, wg_ref, wu_ref, wd_ref, y_hbm,
                x_scr, h_scr, acc_scr, sem_in, sem_out):
    i = pl.program_id(0)
    j = pl.program_id(1)
    tm = MOE_TM
    valid = tv_ref[i] > 0

    @pl.when(jnp.logical_and(i == 0, j == 0))
    def _():
        acc_scr[...] = jnp.zeros_like(acc_scr)
        dump = pltpu.make_async_copy(acc_scr, y_hbm.at[pl.ds(y_hbm.shape[0] - tm, tm)], sem_out)
        dump.start()
        dump.wait()

    @pl.when(jnp.logical_and(valid, j == 0))
    def _():
        def issue(r, carry):
            pltpu.make_async_copy(h_hbm.at[pl.ds(src_ref[0, 0, r], 1)], x_scr.at[pl.ds(r, 1)], sem_in).start()
            return carry
        lax.fori_loop(0, tm, issue, 0)

        def drain(r, carry):
            pltpu.make_async_copy(h_hbm.at[pl.ds(0, 1)], x_scr.at[pl.ds(r, 1)], sem_in).wait()
            return carry
        lax.fori_loop(0, tm, drain, 0)
        h_scr[...] = x_scr[...].astype(BF16)

    @pl.when(valid)
    def _():
        contrib = _swiglu_step(h_scr[...], wg_ref[0], wu_ref[0], wd_ref[0])

        @pl.when(j == 0)
        def _():
            acc_scr[...] = contrib

        @pl.when(j > 0)
        def _():
            acc_scr[...] += contrib

    @pl.when(jnp.logical_and(valid, j == pl.num_programs(1) - 1))
    def _():
        def issue(r, carry):
            pltpu.make_async_copy(acc_scr.at[pl.ds(r, 1)], y_hbm.at[pl.ds(dst_ref[0, 0, r], 1)], sem_out).start()
            return carry
        lax.fori_loop(0, tm, issue, 0)

        def drain(r, carry):
            pltpu.make_async_copy(acc_scr.at[pl.ds(r, 1)], y_hbm.at[pl.ds(0, 1)], sem_out).wait()
            return carry
        lax.fori_loop(0, tm, drain, 0)


def _moe_plan(ri, rows):
    tm = MOE_TM
    n_asg = 2 * rows
    n_pad = n_asg + NE * tm
    n_tiles = n_pad // tm
    e_flat = jnp.concatenate([ri[:, 0], ri[:, 1]])
    onehot = (e_flat[:, None] == jnp.arange(NE, dtype=jnp.int32)[None, :]).astype(jnp.int32)
    rank = jnp.take_along_axis(jnp.cumsum(onehot, axis=0), e_flat[:, None], axis=1)[:, 0] - 1
    counts = jnp.sum(onehot, axis=0)
    gsz = ((counts + tm - 1) // tm) * tm
    gend = jnp.cumsum(gsz)
    gstart = gend - gsz
    pos = gstart[e_flat] + rank
    tok = jnp.arange(n_asg, dtype=jnp.int32) % rows
    src = jnp.zeros((n_pad,), jnp.int32).at[pos].set(tok)
    dump = n_asg + (jnp.arange(n_pad, dtype=jnp.int32) % tm)
    dst = dump.at[pos].set(jnp.arange(n_asg, dtype=jnp.int32))
    tile_row0 = jnp.arange(n_tiles, dtype=jnp.int32) * tm
    tile_valid = (tile_row0 < gend[NE - 1]).astype(jnp.int32)
    tile_e = jnp.sum((tile_row0[:, None] >= gend[None, :]).astype(jnp.int32), axis=1)
    last_e = jnp.sum((gend[NE - 1] - 1 >= gend).astype(jnp.int32))
    tile_e = jnp.where(tile_valid > 0, tile_e, last_e).astype(jnp.int32)
    return tile_e, tile_valid, src.reshape(n_tiles, 1, tm), dst.reshape(n_tiles, 1, tm)


def _moe_experts(h, plan, wg, wu, wd, rows):
    tm, tf = MOE_TM, MOE_TF
    tile_e, tile_valid, src, dst = plan
    n_tiles = src.shape[0]
    nj = FF // tf
    jmap = lambda j, tv, i: jnp.where(tv[i] > 0, j, nj - 1)
    grid_spec = pltpu.PrefetchScalarGridSpec(
        num_scalar_prefetch=2,
        grid=(n_tiles, nj),
        in_specs=[
            pl.BlockSpec((1, 1, tm), lambda i, j, te, tv: (i, 0, 0), memory_space=pltpu.SMEM),
            pl.BlockSpec((1, 1, tm), lambda i, j, te, tv: (i, 0, 0), memory_space=pltpu.SMEM),
            pl.BlockSpec(memory_space=pl.ANY),
            pl.BlockSpec((1, D, tf), lambda i, j, te, tv: (te[i], 0, jmap(j, tv, i))),
            pl.BlockSpec((1, D, tf), lambda i, j, te, tv: (te[i], 0, jmap(j, tv, i))),
            pl.BlockSpec((1, tf, D), lambda i, j, te, tv: (te[i], jmap(j, tv, i), 0)),
        ],
        out_specs=pl.BlockSpec(memory_space=pl.ANY),
        scratch_shapes=[
            pltpu.VMEM((tm, D), F32),
            pltpu.VMEM((tm, D), BF16),
            pltpu.VMEM((tm, D), F32),
            pltpu.SemaphoreType.DMA(()),
            pltpu.SemaphoreType.DMA(()),
        ],
    )
    return pl.pallas_call(
        _moe_kernel,
        out_shape=jax.ShapeDtypeStruct((2 * rows + tm, D), F32),
        grid_spec=grid_spec,
        compiler_params=_cparams(("arbitrary", "arbitrary")),
        name="moe_experts",
    )(tile_e, tile_valid, src, dst, h, wg, wu, wd)


POST_TM = 512


def _moe_post_kernel(x_ref, y0_ref, y1_ref, rw_ref, gpost_ref, gate_ref, o_ref):
    w = rw_ref[...]
    f = w[:, 0:1] * y0_ref[...] + w[:, 1:2] * y1_ref[...]
    o_ref[...] = x_ref[...] + gate_ref[0] * (_rms(f) * gpost_ref[...])


def _moe_post(x, y2, rw, gpost, mod, rows):
    tm = POST_TM
    off = rows // tm
    return pl.pallas_call(
        _moe_post_kernel,
        out_shape=jax.ShapeDtypeStruct((rows, D), F32),
        grid=(rows // tm,),
        in_specs=[
            pl.BlockSpec((tm, D), lambda i: (i, 0)),
            pl.BlockSpec((tm, D), lambda i: (i, 0)),
            pl.BlockSpec((tm, D), lambda i: (off + i, 0)),
            pl.BlockSpec((tm, LANES), lambda i: (i, 0)),
            pl.BlockSpec((1, D), lambda i: (0, 0)),
            _mod_spec(tm, 5),
        ],
        out_specs=pl.BlockSpec((tm, D), lambda i: (i, 0)),
        compiler_params=_cparams(("arbitrary",)),
        name="moe_post",
    )(x, y2, y2, rw, gpost.reshape(1, D), mod)


def _rope_tables():
    rows = SEQ // GRID_W
    row = jnp.repeat(jnp.arange(rows), GRID_W).astype(F32)
    col = jnp.tile(jnp.arange(GRID_W), rows).astype(F32)
    inv = ROPE_BASE ** (-jnp.arange(ROPE_PAIRS, dtype=F32) / ROPE_PAIRS)
    ang_r = row[:, None] * inv[None, :]
    ang_c = col[:, None] * inv[None, :]
    cr, sr, cc, sc = jnp.cos(ang_r), jnp.sin(ang_r), jnp.cos(ang_c), jnp.sin(ang_c)
    cos_t = jnp.concatenate([cr, cr, cc, cc], axis=-1)
    sin_t = jnp.concatenate([-sr, sr, -sc, sc], axis=-1)
    cos_t = jnp.concatenate([cos_t, jnp.ones((IN_TM, LANES), F32)], axis=0)
    sin_t = jnp.concatenate([sin_t, jnp.zeros((IN_TM, LANES), F32)], axis=0)
    return cos_t, sin_t


def _permute_in_cols(w):
    qa, ka, va, qb, kb, vb = (w[:, 0:1024], w[:, 1024:1280], w[:, 1280:1536],
                              w[:, 1536:2560], w[:, 2560:3584], w[:, 3584:4608])
    return jnp.concatenate([qa, qb, kb, vb, ka, va], axis=1)


def kernel(x, c, ctx, c_ctx, w_mod, b_mod, g_attn_pre, g_attn_post, g_ffn_pre, g_ffn_post, w_in, w_out,
           sink_logit, lambda_q1, lambda_k1, lambda_q2, lambda_k2, subln_g, ffn_w_gate, ffn_w_up,
           ffn_w_down, moe_router, moe_w_gate, moe_w_up, moe_w_down):
    xs = jnp.concatenate([x.reshape(NL, D), ctx.reshape(NC, D)], axis=0)
    cin = jnp.concatenate([c, c_ctx[None, :], jnp.zeros((8 - B - 1, D), F32)], axis=0)
    mod_all = _modulation(cin, w_mod, b_mod)
    cos_t, sin_t = _rope_tables()

    for l in range(DEPTH):
        last = l == DEPTH - 1
        rows = NL if last else NR
        mod = mod_all[l].reshape(8, 1, N_MOD * D)
        lam_init = 0.8 - 0.6 * math.exp(-0.3 * l)
        lam_vecs = jnp.stack([lambda_q1[l], lambda_k1[l], lambda_q2[l], lambda_k2[l]])

        qkv = _in_proj(xs, g_attn_pre[l], mod, _permute_in_cols(w_in[l]).astype(BF16), cos_t, sin_t)
        oa = _attn_a(qkv, sink_logit[l], local=True)
        ob = _attn_b(qkv, lam_vecs, subln_g[l], lam_init, with_lat=True)
        if not last:
            oa = jnp.concatenate([oa, _attn_a(qkv, sink_logit[l], local=False)], axis=0)
            ob = jnp.concatenate([ob, _attn_b(qkv, lam_vecs, subln_g[l], lam_init, with_lat=False)], axis=0)
        xs = _out_proj(xs, oa, ob, w_out[l].astype(BF16), g_attn_post[l], mod, rows)

        i = l // 2
        if l % 2 == 0:
            xs = _ffn_dense(xs, g_ffn_pre[l], g_ffn_post[l], mod, ffn_w_gate[i].astype(BF16),
                            ffn_w_up[i].astype(BF16), ffn_w_down[i].astype(BF16), rows)
        else:
            wr_pad = jnp.pad(moe_router[i], ((0, 0), (0, LANES - NE)))
            h, ri, rw = _router(xs, g_ffn_pre[l], mod, wr_pad, rows)
            plan = _moe_plan(ri, rows)
            y2 = _moe_experts(h, plan, moe_w_gate[i].astype(BF16), moe_w_up[i].astype(BF16),
                              moe_w_down[i].astype(BF16), rows)
            xs = _moe_post(xs, y2, rw, g_ffn_post[l], mod, rows)
    return xs.reshape(B, SEQ, D)
```

```python
import functools
import math

import jax
import jax.numpy as jnp
from jax import lax
from jax.experimental import pallas as pl
from jax.experimental.pallas import tpu as pltpu

D = 2048
B = 4
SEQ = 2048
DEPTH = 4
GRID_W = 64
CTX = 256
HD = 128
WINDOW = 128
A_HEADS = 8
A_KVH = 2
A_GROUP = 4
B_HEADS = 4
B_VDIM = 256
FF = 5632
NE = 8
N_MOD = 6
ROPE_BASE = 10000.0
ROPE_PAIRS = 32
ATTN_SCALE = HD ** -0.5
EPS = 1e-6
NEG = -1e30

NL = B * SEQ
NC = B * CTX
NR = NL + NC
IN_COLS = 4608
COL_QA, COL_QB, COL_KB, COL_VB, COL_KA, COL_VA = 0, 1024, 2048, 3072, 4096, 4352

LANES = 128
VMEM_LIMIT = 56 * 1024 * 1024

F32 = jnp.float32
BF16 = jnp.bfloat16
HIGHEST = lax.Precision.HIGHEST


def _cparams(sem):
    return pltpu.CompilerParams(dimension_semantics=sem, vmem_limit_bytes=VMEM_LIMIT)


def _nt_dot(a, b):
    return lax.dot_general(a, b, (((1,), (1,)), ((), ())), preferred_element_type=F32)


def _rms(x):
    return x * lax.rsqrt(jnp.mean(x * x, axis=-1, keepdims=True) + EPS)


def _norm_mod(x, g, sh, sc):
    return (_rms(x) * g) * (1.0 + sc) + sh


def _mod_row(i, tm):
    return jnp.where(i < NL // tm, (i * tm) // SEQ, B)


def _mod_spec(tm, chunk, grid_rank=1):
    if grid_rank == 1:
        return pl.BlockSpec((1, 1, D), lambda i: (_mod_row(i, tm), 0, chunk))
    return pl.BlockSpec((1, 1, D), lambda i, j: (_mod_row(i, tm), 0, chunk))


MOD_TN = 1024


def _mod_kernel(c_ref, w_ref, b_ref, o_ref):
    c = c_ref[...]
    a = c * jax.nn.sigmoid(c)
    o_ref[0] = jnp.dot(a, w_ref[0], precision=HIGHEST, preferred_element_type=F32) + b_ref[0]


def _modulation(cin, w_mod, b_mod):
    return pl.pallas_call(
        _mod_kernel,
        out_shape=jax.ShapeDtypeStruct((DEPTH, 8, N_MOD * D), F32),
        grid=(DEPTH, N_MOD * D // MOD_TN),
        in_specs=[
            pl.BlockSpec((8, D), lambda l, n: (0, 0)),
            pl.BlockSpec((1, D, MOD_TN), lambda l, n: (l, 0, n)),
            pl.BlockSpec((1, 1, MOD_TN), lambda l, n: (l, 0, n)),
        ],
        out_specs=pl.BlockSpec((1, 8, MOD_TN), lambda l, n: (l, 0, n)),
        compiler_params=_cparams(("arbitrary", "arbitrary")),
        name="modulation",
    )(cin, w_mod, b_mod.reshape(DEPTH, 1, N_MOD * D))


IN_TM = 256
IN_CH = 512


def _inproj_kernel(x_ref, g_ref, sh_ref, sc_ref, cos_ref, sin_ref, w_ref, o_ref):
    hb = _norm_mod(x_ref[...], g_ref[...], sh_ref[0], sc_ref[0]).astype(BF16)
    cos = cos_ref[...]
    sin = sin_ref[...]
    lane = lax.broadcasted_iota(jnp.int32, (IN_TM, LANES), 1)
    first = (lane & 63) < 32
    for c in range(IN_COLS // IN_CH):
        acc = jnp.dot(hb, w_ref[:, c * IN_CH:(c + 1) * IN_CH], preferred_element_type=F32)
        for s in range(IN_CH // LANES):
            col = c * IN_CH + s * LANES
            y = acc[:, s * LANES:(s + 1) * LANES]
            if col < COL_VB or COL_KA <= col < COL_VA:
                partner = jnp.where(first, pltpu.roll(y, 96, 1), pltpu.roll(y, 32, 1))
                y = y * cos + partner * sin
                if col < COL_KB:
                    y = y * ATTN_SCALE
            o_ref[:, col:col + LANES] = y.astype(BF16)


def _in_proj(x, g, mod, w_bf, cos_t, sin_t):
    rows = x.shape[0]
    tm = IN_TM
    tab = lambda i: (jnp.where(i < NL // tm, i % (SEQ // tm), SEQ // tm), 0)
    return pl.pallas_call(
        _inproj_kernel,
        out_shape=jax.ShapeDtypeStruct((rows, IN_COLS), BF16),
        grid=(rows // tm,),
        in_specs=[
            pl.BlockSpec((tm, D), lambda i: (i, 0)),
            pl.BlockSpec((1, D), lambda i: (0, 0)),
            _mod_spec(tm, 0),
            _mod_spec(tm, 1),
            pl.BlockSpec((tm, LANES), tab),
            pl.BlockSpec((tm, LANES), tab),
            pl.BlockSpec((D, IN_COLS), lambda i: (0, 0)),
        ],
        out_specs=pl.BlockSpec((tm, IN_COLS), lambda i: (i, 0)),
        compiler_params=_cparams(("arbitrary",)),
        name="in_proj",
    )(x, g.reshape(1, D), mod, mod, cos_t, sin_t, w_bf)


ATT_TQ = 256
A_WIN = ATT_TQ + 2 * WINDOW


def _attn_a_kernel(sink_ref, q_ref, kc_ref, vc_ref, *rest, local):
    if local:
        kl_ref, vl_ref, o_ref = rest
    else:
        (o_ref,) = rest
    tq = ATT_TQ
    i = pl.program_id(1)
    for kv in range(A_KVH):
        ksl = slice(kv * HD, (kv + 1) * HD)
        qs = jnp.concatenate(
            [q_ref[:, (kv * A_GROUP + g) * HD:(kv * A_GROUP + g + 1) * HD] for g in range(A_GROUP)], axis=0)
        sink = jnp.concatenate(
            [jnp.full((tq, 1), sink_ref[kv * A_GROUP + g], F32) for g in range(A_GROUP)], axis=0)
        s_ctx = _nt_dot(qs, kc_ref[:, ksl])
        m = jnp.maximum(jnp.max(s_ctx, axis=-1, keepdims=True), sink)
        if local:
            st = pl.multiple_of(jnp.clip(i * tq - WINDOW, 0, SEQ - A_WIN), WINDOW)
            s_loc = _nt_dot(qs, kl_ref[pl.ds(st, A_WIN), ksl])
            qpos = i * tq + (lax.broadcasted_iota(jnp.int32, (A_GROUP * tq, A_WIN), 0) & (tq - 1))
            kpos = st + lax.broadcasted_iota(jnp.int32, (A_GROUP * tq, A_WIN), 1)
            s_loc = jnp.where(jnp.abs(qpos - kpos) <= WINDOW, s_loc, NEG)
            m = jnp.maximum(m, jnp.max(s_loc, axis=-1, keepdims=True))
        p_ctx = jnp.exp(s_ctx - m)
        den = jnp.sum(p_ctx, axis=-1, keepdims=True) + jnp.exp(sink - m)
        o = jnp.dot(p_ctx.astype(BF16), vc_ref[:, ksl], preferred_element_type=F32)
        if local:
            p_loc = jnp.exp(s_loc - m)
            den = den + jnp.sum(p_loc, axis=-1, keepdims=True)
            o = o + jnp.dot(p_loc.astype(BF16), vl_ref[pl.ds(st, A_WIN), ksl], preferred_element_type=F32)
        o = o * (1.0 / den)
        for g in range(A_GROUP):
            h = kv * A_GROUP + g
            o_ref[:, h * HD:(h + 1) * HD] = o[g * tq:(g + 1) * tq].astype(BF16)


def _attn_a(qkv, sink, local):
    tq = ATT_TQ
    nq = SEQ // tq if local else CTX // tq
    row0 = 0 if local else NL // tq
    per_b = nq
    ctx0 = NL // CTX
    in_specs = [
        pl.BlockSpec(memory_space=pltpu.SMEM),
        pl.BlockSpec((tq, 1024), lambda b, i: (row0 + b * per_b + i, COL_QA // 1024)),
        pl.BlockSpec((CTX, 256), lambda b, i: (ctx0 + b, COL_KA // 256)),
        pl.BlockSpec((CTX, 256), lambda b, i: (ctx0 + b, COL_VA // 256)),
    ]
    args = [sink.reshape(A_HEADS), qkv, qkv, qkv]
    if local:
        in_specs += [
            pl.BlockSpec((SEQ, 256), lambda b, i: (b, COL_KA // 256)),
            pl.BlockSpec((SEQ, 256), lambda b, i: (b, COL_VA // 256)),
        ]
        args += [qkv, qkv]
    rows = NL if local else NC
    return pl.pallas_call(
        functools.partial(_attn_a_kernel, local=local),
        out_shape=jax.ShapeDtypeStruct((rows, 1024), BF16),
        grid=(B, nq),
        in_specs=in_specs,
        out_specs=pl.BlockSpec((tq, 1024), lambda b, i: (b * per_b + i, 0)),
        compiler_params=_cparams(("arbitrary", "arbitrary")),
        name="attn_a_lat" if local else "attn_a_ctx",
    )(*args)


def _attn_b_kernel(lam_ref, g_ref, q_ref, kc_ref, vc_ref, *rest, with_lat, lam_init):
    if with_lat:
        kl_ref, vl_ref, o_ref = rest
    else:
        (o_ref,) = rest
    lv = lam_ref[...]
    lam = (jnp.exp(jnp.sum(lv[0:1] * lv[1:2], axis=-1, keepdims=True))
           - jnp.exp(jnp.sum(lv[2:3] * lv[3:4], axis=-1, keepdims=True)) + lam_init)
    for h in range(B_HEADS):
        pc, plat = [], []
        for mth in range(2):
            sl = slice((2 * h + mth) * HD, (2 * h + mth + 1) * HD)
            q = q_ref[:, sl]
            s_c = _nt_dot(q, kc_ref[:, sl])
            mx = jnp.max(s_c, axis=-1, keepdims=True)
            if with_lat:
                s_l = _nt_dot(q, kl_ref[:, sl])
                mx = jnp.maximum(mx, jnp.max(s_l, axis=-1, keepdims=True))
            e_c = jnp.exp(s_c - mx)
            den = jnp.sum(e_c, axis=-1, keepdims=True)
            if with_lat:
                e_l = jnp.exp(s_l - mx)
                den = den + jnp.sum(e_l, axis=-1, keepdims=True)
            r = 1.0 / den
            pc.append(e_c * r)
            if with_lat:
                plat.append(e_l * r)
        vsl = slice(h * B_VDIM, (h + 1) * B_VDIM)
        o = jnp.dot((pc[0] - lam * pc[1]).astype(BF16), vc_ref[:, vsl], preferred_element_type=F32)
        if with_lat:
            o = o + jnp.dot((plat[0] - lam * plat[1]).astype(BF16), vl_ref[:, vsl],
                            preferred_element_type=F32)
        o = (_rms(o) * g_ref[...]) * (1.0 - lam_init)
        o_ref[:, vsl] = o.astype(BF16)


def _attn_b(qkv, lam_vecs, subln_g, lam_init, with_lat):
    tq = ATT_TQ
    nq = SEQ // tq if with_lat else CTX // tq
    row0 = 0 if with_lat else NL // tq
    ctx0 = NL // CTX
    in_specs = [
        pl.BlockSpec((4, HD), lambda b, i: (0, 0)),
        pl.BlockSpec((1, B_VDIM), lambda b, i: (0, 0)),
        pl.BlockSpec((tq, 1024), lambda b, i: (row0 + b * nq + i, COL_QB // 1024)),
        pl.BlockSpec((CTX, 1024), lambda b, i: (ctx0 + b, COL_KB // 1024)),
        pl.BlockSpec((CTX, 1024), lambda b, i: (ctx0 + b, COL_VB // 1024)),
    ]
    args = [lam_vecs, subln_g.reshape(1, B_VDIM), qkv, qkv, qkv]
    if with_lat:
        in_specs += [
            pl.BlockSpec((SEQ, 1024), lambda b, i: (b, COL_KB // 1024)),
            pl.BlockSpec((SEQ, 1024), lambda b, i: (b, COL_VB // 1024)),
        ]
        args += [qkv, qkv]
    rows = NL if with_lat else NC
    return pl.pallas_call(
        functools.partial(_attn_b_kernel, with_lat=with_lat, lam_init=lam_init),
        out_shape=jax.ShapeDtypeStruct((rows, 1024), BF16),
        grid=(B, nq),
        in_specs=in_specs,
        out_specs=pl.BlockSpec((tq, 1024), lambda b, i: (b * nq + i, 0)),
        compiler_params=_cparams(("arbitrary", "arbitrary")),
        name="attn_b_lat" if with_lat else "attn_b_ctx",
    )(*args)


OUT_TM = 512


def _outproj_kernel(x_ref, oa_ref, ob_ref, w_ref, g_ref, gate_ref, o_ref):
    y = (jnp.dot(oa_ref[...], w_ref[0:1024, :], preferred_element_type=F32)
         + jnp.dot(ob_ref[...], w_ref[1024:2048, :], preferred_element_type=F32))
    o_ref[...] = x_ref[...] + gate_ref[0] * (_rms(y) * g_ref[...])


def _out_proj(x, oa, ob, w_bf, g, mod, rows):
    tm = OUT_TM
    return pl.pallas_call(
        _outproj_kernel,
        out_shape=jax.ShapeDtypeStruct((rows, D), F32),
        grid=(rows // tm,),
        in_specs=[
            pl.BlockSpec((tm, D), lambda i: (i, 0)),
            pl.BlockSpec((tm, 1024), lambda i: (i, 0)),
            pl.BlockSpec((tm, 1024), lambda i: (i, 0)),
            pl.BlockSpec((D, D), lambda i: (0, 0)),
            pl.BlockSpec((1, D), lambda i: (0, 0)),
            _mod_spec(tm, 2),
        ],
        out_specs=pl.BlockSpec((tm, D), lambda i: (i, 0)),
        compiler_params=_cparams(("arbitrary",)),
        name="out_proj",
    )(x, oa, ob, w_bf, g.reshape(1, D), mod)


FFN_TM = 512
FFN_TF = 512


def _swiglu_step(hb, wg, wu, wd):
    gt = jnp.dot(hb, wg, preferred_element_type=F32)
    up = jnp.dot(hb, wu, preferred_element_type=F32)
    act = (gt * jax.nn.sigmoid(gt) * up).astype(BF16)
    return jnp.dot(act, wd, preferred_element_type=F32)


def _ffn_kernel(x_ref, gpre_ref, sh_ref, sc_ref, gate_ref, gpost_ref, wg_ref, wu_ref, wd_ref, o_ref,
                h_scr, acc_scr):
    j = pl.program_id(1)

    @pl.when(j == 0)
    def _():
        h_scr[...] = _norm_mod(x_ref[...], gpre_ref[...], sh_ref[0], sc_ref[0]).astype(BF16)
        acc_scr[...] = jnp.zeros_like(acc_scr)

    acc_scr[...] += _swiglu_step(h_scr[...], wg_ref[...], wu_ref[...], wd_ref[...])

    @pl.when(j == pl.num_programs(1) - 1)
    def _():
        o_ref[...] = x_ref[...] + gate_ref[0] * (_rms(acc_scr[...]) * gpost_ref[...])


def _ffn_dense(x, gpre, gpost, mod, wg, wu, wd, rows):
    tm, tf = FFN_TM, FFN_TF
    return pl.pallas_call(
        _ffn_kernel,
        out_shape=jax.ShapeDtypeStruct((rows, D), F32),
        grid=(rows // tm, FF // tf),
        in_specs=[
            pl.BlockSpec((tm, D), lambda i, j: (i, 0)),
            pl.BlockSpec((1, D), lambda i, j: (0, 0)),
            _mod_spec(tm, 3, 2),
            _mod_spec(tm, 4, 2),
            _mod_spec(tm, 5, 2),
            pl.BlockSpec((1, D), lambda i, j: (0, 0)),
            pl.BlockSpec((D, tf), lambda i, j: (0, j)),
            pl.BlockSpec((D, tf), lambda i, j: (0, j)),
            pl.BlockSpec((tf, D), lambda i, j: (j, 0)),
        ],
        out_specs=pl.BlockSpec((tm, D), lambda i, j: (i, 0)),
        scratch_shapes=[pltpu.VMEM((tm, D), BF16), pltpu.VMEM((tm, D), F32)],
        compiler_params=_cparams(("arbitrary", "arbitrary")),
        name="ffn_dense",
    )(x, gpre.reshape(1, D), mod, mod, mod, gpost.reshape(1, D), wg, wu, wd)


RT_TM = 512
MOE_SUB = 256
MOE_RS = 2560
MOE_TF = 256


def _router_kernel(x_ref, gpre_ref, sh_ref, sc_ref, wr_ref, h_ref, ri_ref, rw_ref):
    h = _norm_mod(x_ref[...], gpre_ref[...], sh_ref[0], sc_ref[0])
    h_ref[...] = h
    logits = jnp.dot(h, wr_ref[...], precision=HIGHEST, preferred_element_type=F32)
    lane = lax.broadcasted_iota(jnp.int32, logits.shape, 1)
    logits = jnp.where(lane < NE, logits, -jnp.inf)
    m1 = jnp.max(logits, axis=-1, keepdims=True)
    i1 = jnp.min(jnp.where(logits == m1, lane, LANES), axis=-1, keepdims=True)
    rest = jnp.where(lane == i1, -jnp.inf, logits)
    m2 = jnp.max(rest, axis=-1, keepdims=True)
    i2 = jnp.min(jnp.where(rest == m2, lane, LANES), axis=-1, keepdims=True)
    e2 = jnp.exp(m2 - m1)
    den = 1.0 + e2
    ri_ref[...] = jnp.where(lane == 0, i1, jnp.where(lane == 1, i2, 0))
    rw_ref[...] = jnp.where(lane == 0, 1.0 / den, jnp.where(lane == 1, e2 / den, 0.0))


def _router(x, gpre, mod, wr_pad, rows):
    tm = RT_TM
    return pl.pallas_call(
        _router_kernel,
        out_shape=(jax.ShapeDtypeStruct((rows, D), F32),
                   jax.ShapeDtypeStruct((rows, LANES), jnp.int32),
                   jax.ShapeDtypeStruct((rows, LANES), F32)),
        grid=(rows // tm,),
        in_specs=[
            pl.BlockSpec((tm, D), lambda i: (i, 0)),
            pl.BlockSpec((1, D), lambda i: (0, 0)),
            _mod_spec(tm, 3),
            _mod_spec(tm, 4),
            pl.BlockSpec((D, LANES), lambda i: (0, 0)),
        ],
        out_specs=(pl.BlockSpec((tm, D), lambda i: (i, 0)),
                   pl.BlockSpec((tm, LANES), lambda i: (i, 0)),
                   pl.BlockSpec((tm, LANES), lambda i: (i, 0))),
        compiler_params=_cparams(("arbitrary",)),
        name="moe_router",
    )(x, gpre.reshape(1, D), mod, mod, wr_pad)


def _moe_kernel(te_ref, nsub_ref, src_ref, dst_ref, h_hbm, wg_ref, wu_ref, wd_ref, y_hbm,
                x_scr, stage, acc_scr, wgb, wub, wdb, sem_in, sem_out):
    s = pl.program_id(0)
    j = pl.program_id(1)
    sub = MOE_SUB
    nsub = nsub_ref[s]
    active = nsub > 0

    def rows_of(c):
        return pl.ds(pl.multiple_of(c * sub, sub), sub)

    @pl.when(jnp.logical_and(s == 0, j == 0))
    def _():
        stage[0] = jnp.zeros((sub, D), F32)
        dump = pltpu.make_async_copy(stage.at[0], y_hbm.at[pl.ds(y_hbm.shape[0] - sub, sub)], sem_out)
        dump.start()
        dump.wait()

    @pl.when(jnp.logical_and(active, j == 0))
    def _():
        def issue(c, slot):
            def one(r, carry):
                pltpu.make_async_copy(h_hbm.at[pl.ds(src_ref[0, 0, c * sub + r], 1)],
                                      stage.at[slot, pl.ds(r, 1)], sem_in.at[slot]).start()
                return carry
            lax.fori_loop(0, sub, one, 0, unroll=8)

        issue(0, 0)

        def chunk(c, carry):
            slot = c & 1

            @pl.when(c + 1 < nsub)
            def _():
                issue(c + 1, 1 - slot)

            pltpu.make_async_copy(h_hbm.at[pl.ds(0, sub)], stage.at[slot], sem_in.at[slot]).wait()
            x_scr[rows_of(c)] = stage[slot].astype(BF16)
            acc_scr[rows_of(c)] = jnp.zeros((sub, D), F32)
            return carry
        lax.fori_loop(0, nsub, chunk, 0)

    @pl.when(active)
    def _():
        wgb[...] = wg_ref[0, 0].astype(BF16)
        wub[...] = wu_ref[0, 0].astype(BF16)
        wdb[...] = wd_ref[0, 0].astype(BF16)

        def pair(p, carry):
            rows = pl.ds(pl.multiple_of(p * (2 * sub), 2 * sub), 2 * sub)
            acc_scr[rows] += _swiglu_step(x_scr[rows], wgb[...], wub[...], wdb[...])
            return carry
        lax.fori_loop(0, nsub >> 1, pair, 0)

        @pl.when((nsub & 1) == 1)
        def _():
            rows = rows_of(nsub - 1)
            acc_scr[rows] += _swiglu_step(x_scr[rows], wgb[...], wub[...], wdb[...])

    @pl.when(jnp.logical_and(active, j == pl.num_programs(1) - 1))
    def _():
        def issue(c, carry):
            def one(r, carry2):
                row = c * sub + r
                pltpu.make_async_copy(acc_scr.at[pl.ds(row, 1)], y_hbm.at[pl.ds(dst_ref[0, 0, row], 1)],
                                      sem_out).start()
                return carry2
            lax.fori_loop(0, sub, one, 0, unroll=8)
            return carry
        lax.fori_loop(0, nsub, issue, 0)

        def drain(c, carry):
            pltpu.make_async_copy(acc_scr.at[pl.ds(0, sub)], y_hbm.at[pl.ds(0, sub)], sem_out).wait()
            return carry
        lax.fori_loop(0, nsub, drain, 0)


def _moe_plan(ri, rows):
    sub, rs = MOE_SUB, MOE_RS
    n_asg = 2 * rows
    ns_max = (n_asg + NE * (sub - 1)) // rs + NE
    e_flat = jnp.concatenate([ri[:, 0], ri[:, 1]])
    _, order = lax.sort((e_flat, jnp.arange(n_asg, dtype=jnp.int32)), num_keys=1)
    counts = jnp.sum((e_flat[:, None] == jnp.arange(NE, dtype=jnp.int32)[None, :]).astype(jnp.int32), axis=0)
    ustart = jnp.cumsum(counts) - counts
    psz = ((counts + sub - 1) // sub) * sub
    nst = (psz + rs - 1) // rs
    st_end = jnp.cumsum(nst)
    total = st_end[NE - 1]
    s_idx = jnp.arange(ns_max, dtype=jnp.int32)
    valid = s_idx < total
    s_eff = jnp.minimum(s_idx, total - 1)
    e_s = jnp.sum((s_eff[:, None] >= st_end[None, :]).astype(jnp.int32), axis=1)
    k_s = s_eff - (st_end - nst)[e_s]
    nsub = jnp.where(valid, jnp.clip(psz[e_s] - k_s * rs, 0, rs) // sub, 0)
    lane = jnp.arange(rs, dtype=jnp.int32)[None, :]
    rank = k_s[:, None] * rs + lane
    row_valid = valid[:, None] & (rank < counts[e_s][:, None])
    asg = order[jnp.clip(ustart[e_s][:, None] + rank, 0, n_asg - 1)]
    src = jnp.where(row_valid, asg % rows, 0)
    dst = jnp.where(row_valid, asg, n_asg + lane % sub)
    return (e_s.astype(jnp.int32), nsub.astype(jnp.int32),
            src.astype(jnp.int32).reshape(ns_max, 1, rs), dst.astype(jnp.int32).reshape(ns_max, 1, rs))


def _moe_experts(h, plan, wg, wu, wd, layer, rows):
    sub, rs, tf = MOE_SUB, MOE_RS, MOE_TF
    tile_e, nsub, src, dst = plan
    ns_max = src.shape[0]
    nj = FF // tf
    jmap = lambda j, ns, s: jnp.where(ns[s] > 0, j, nj - 1)
    grid_spec = pltpu.PrefetchScalarGridSpec(
        num_scalar_prefetch=2,
        grid=(ns_max, nj),
        in_specs=[
            pl.BlockSpec((1, 1, rs), lambda s, j, te, ns: (s, 0, 0), memory_space=pltpu.SMEM),
            pl.BlockSpec((1, 1, rs), lambda s, j, te, ns: (s, 0, 0), memory_space=pltpu.SMEM),
            pl.BlockSpec(memory_space=pl.ANY),
            pl.BlockSpec((1, 1, D, tf), lambda s, j, te, ns: (layer, te[s], 0, jmap(j, ns, s))),
            pl.BlockSpec((1, 1, D, tf), lambda s, j, te, ns: (layer, te[s], 0, jmap(j, ns, s))),
            pl.BlockSpec((1, 1, tf, D), lambda s, j, te, ns: (layer, te[s], jmap(j, ns, s), 0)),
        ],
        out_specs=pl.BlockSpec(memory_space=pl.ANY),
        scratch_shapes=[
            pltpu.VMEM((rs, D), BF16),
            pltpu.VMEM((2, sub, D), F32),
            pltpu.VMEM((rs, D), F32),
            pltpu.VMEM((D, tf), BF16),
            pltpu.VMEM((D, tf), BF16),
            pltpu.VMEM((tf, D), BF16),
            pltpu.SemaphoreType.DMA((2,)),
            pltpu.SemaphoreType.DMA(()),
        ],
    )
    return pl.pallas_call(
        _moe_kernel,
        out_shape=jax.ShapeDtypeStruct((2 * rows + sub, D), F32),
        grid_spec=grid_spec,
        compiler_params=_cparams(("arbitrary", "arbitrary")),
        name="moe_experts",
    )(tile_e, nsub, src, dst, h, wg, wu, wd)


POST_TM = 512


def _moe_post_kernel(x_ref, y0_ref, y1_ref, rw_ref, gpost_ref, gate_ref, o_ref):
    w = rw_ref[...]
    f = w[:, 0:1] * y0_ref[...] + w[:, 1:2] * y1_ref[...]
    o_ref[...] = x_ref[...] + gate_ref[0] * (_rms(f) * gpost_ref[...])


def _moe_post(x, y2, rw, gpost, mod, rows):
    tm = POST_TM
    off = rows // tm
    return pl.pallas_call(
        _moe_post_kernel,
        out_shape=jax.ShapeDtypeStruct((rows, D), F32),
        grid=(rows // tm,),
        in_specs=[
            pl.BlockSpec((tm, D), lambda i: (i, 0)),
            pl.BlockSpec((tm, D), lambda i: (i, 0)),
            pl.BlockSpec((tm, D), lambda i: (off + i, 0)),
            pl.BlockSpec((tm, LANES), lambda i: (i, 0)),
            pl.BlockSpec((1, D), lambda i: (0, 0)),
            _mod_spec(tm, 5),
        ],
        out_specs=pl.BlockSpec((tm, D), lambda i: (i, 0)),
        compiler_params=_cparams(("arbitrary",)),
        name="moe_post",
    )(x, y2, y2, rw, gpost.reshape(1, D), mod)


def _rope_tables():
    rows = SEQ // GRID_W
    row = jnp.repeat(jnp.arange(rows), GRID_W).astype(F32)
    col = jnp.tile(jnp.arange(GRID_W), rows).astype(F32)
    inv = ROPE_BASE ** (-jnp.arange(ROPE_PAIRS, dtype=F32) / ROPE_PAIRS)
    ang_r = row[:, None] * inv[None, :]
    ang_c = col[:, None] * inv[None, :]
    cr, sr, cc, sc = jnp.cos(ang_r), jnp.sin(ang_r), jnp.cos(ang_c), jnp.sin(ang_c)
    cos_t = jnp.concatenate([cr, cr, cc, cc], axis=-1)
    sin_t = jnp.concatenate([-sr, sr, -sc, sc], axis=-1)
    cos_t = jnp.concatenate([cos_t, jnp.ones((IN_TM, LANES), F32)], axis=0)
    sin_t = jnp.concatenate([sin_t, jnp.zeros((IN_TM, LANES), F32)], axis=0)
    return cos_t, sin_t


def _permute_in_cols(w):
    qa, ka, va, qb, kb, vb = (w[:, 0:1024], w[:, 1024:1280], w[:, 1280:1536],
                              w[:, 1536:2560], w[:, 2560:3584], w[:, 3584:4608])
    return jnp.concatenate([qa, qb, kb, vb, ka, va], axis=1)


def kernel(x, c, ctx, c_ctx, w_mod, b_mod, g_attn_pre, g_attn_post, g_ffn_pre, g_ffn_post, w_in, w_out,
           sink_logit, lambda_q1, lambda_k1, lambda_q2, lambda_k2, subln_g, ffn_w_gate, ffn_w_up,
           ffn_w_down, moe_router, moe_w_gate, moe_w_up, moe_w_down):
    xs = jnp.concatenate([x.reshape(NL, D), ctx.reshape(NC, D)], axis=0)
    cin = jnp.concatenate([c, c_ctx[None, :], jnp.zeros((8 - B - 1, D), F32)], axis=0)
    mod_all = _modulation(cin, w_mod, b_mod)
    cos_t, sin_t = _rope_tables()

    for l in range(DEPTH):
        last = l == DEPTH - 1
        rows = NL if last else NR
        mod = mod_all[l].reshape(8, 1, N_MOD * D)
        lam_init = 0.8 - 0.6 * math.exp(-0.3 * l)
        lam_vecs = jnp.stack([lambda_q1[l], lambda_k1[l], lambda_q2[l], lambda_k2[l]])

        qkv = _in_proj(xs, g_attn_pre[l], mod, _permute_in_cols(w_in[l]).astype(BF16), cos_t, sin_t)
        oa = _attn_a(qkv, sink_logit[l], local=True)
        ob = _attn_b(qkv, lam_vecs, subln_g[l], lam_init, with_lat=True)
        if not last:
            oa = jnp.concatenate([oa, _attn_a(qkv, sink_logit[l], local=False)], axis=0)
            ob = jnp.concatenate([ob, _attn_b(qkv, lam_vecs, subln_g[l], lam_init, with_lat=False)], axis=0)
        xs = _out_proj(xs, oa, ob, w_out[l].astype(BF16), g_attn_post[l], mod, rows)

        i = l // 2
        if l % 2 == 0:
            xs = _ffn_dense(xs, g_ffn_pre[l], g_ffn_post[l], mod, ffn_w_gate[i].astype(BF16),
                            ffn_w_up[i].astype(BF16), ffn_w_down[i].astype(BF16), rows)
        else:
            wr_pad = jnp.pad(moe_router[i], ((0, 0), (0, LANES - NE)))
            h, ri, rw = _router(xs, g_ffn_pre[l], mod, wr_pad, rows)
            plan = _moe_plan(ri, rows)
            y2 = _moe_experts(h, plan, moe_w_gate, moe_w_up, moe_w_down, i, rows)
            xs = _moe_post(xs, y2, rw, g_ffn_post[l], mod, rows)
    return xs.reshape(B, SEQ, D)
```

```python
import functools
import math

import jax
import jax.numpy as jnp
from jax import lax
from jax.experimental import pallas as pl
from jax.experimental.pallas import tpu as pltpu

D = 2048
B = 4
SEQ = 2048
DEPTH = 4
GRID_W = 64
CTX = 256
HD = 128
WINDOW = 128
A_HEADS = 8
A_KVH = 2
A_GROUP = 4
B_HEADS = 4
B_VDIM = 256
FF = 5632
NE = 8
N_MOD = 6
ROPE_BASE = 10000.0
ROPE_PAIRS = 32
ATTN_SCALE = HD ** -0.5
EPS = 1e-6
NEG = -1e30

NL = B * SEQ
NC = B * CTX
NR = NL + NC
IN_COLS = 4608
COL_QA, COL_QB, COL_KB, COL_VB, COL_KA, COL_VA = 0, 1024, 2048, 3072, 4096, 4352

LANES = 128
VMEM_LIMIT = 56 * 1024 * 1024

F32 = jnp.float32
BF16 = jnp.bfloat16
HIGHEST = lax.Precision.HIGHEST


def _cparams(sem):
    return pltpu.CompilerParams(dimension_semantics=sem, vmem_limit_bytes=VMEM_LIMIT)


def _nt_dot(a, b):
    return lax.dot_general(a, b, (((1,), (1,)), ((), ())), preferred_element_type=F32)


def _rms(x):
    return x * lax.rsqrt(jnp.mean(x * x, axis=-1, keepdims=True) + EPS)


def _norm_mod(x, g, sh, sc):
    return (_rms(x) * g) * (1.0 + sc) + sh


def _mod_row(i, tm):
    return jnp.where(i < NL // tm, (i * tm) // SEQ, B)


def _mod_spec(tm, chunk, grid_rank=1):
    if grid_rank == 1:
        return pl.BlockSpec((1, 1, D), lambda i: (_mod_row(i, tm), 0, chunk))
    return pl.BlockSpec((1, 1, D), lambda i, j: (_mod_row(i, tm), 0, chunk))


MOD_TN = 1024


def _mod_kernel(c_ref, w_ref, b_ref, o_ref):
    c = c_ref[...]
    a = c * jax.nn.sigmoid(c)
    o_ref[0] = jnp.dot(a, w_ref[0], precision=HIGHEST, preferred_element_type=F32) + b_ref[0]


def _modulation(cin, w_mod, b_mod):
    return pl.pallas_call(
        _mod_kernel,
        out_shape=jax.ShapeDtypeStruct((DEPTH, 8, N_MOD * D), F32),
        grid=(DEPTH, N_MOD * D // MOD_TN),
        in_specs=[
            pl.BlockSpec((8, D), lambda l, n: (0, 0)),
            pl.BlockSpec((1, D, MOD_TN), lambda l, n: (l, 0, n)),
            pl.BlockSpec((1, 1, MOD_TN), lambda l, n: (l, 0, n)),
        ],
        out_specs=pl.BlockSpec((1, 8, MOD_TN), lambda l, n: (l, 0, n)),
        compiler_params=_cparams(("arbitrary", "arbitrary")),
        name="modulation",
    )(cin, w_mod, b_mod.reshape(DEPTH, 1, N_MOD * D))


IN_TM = 256
IN_CH = 512
LOG2E = math.log2(math.e)
Q_SCALE = ATTN_SCALE * LOG2E
IN_CHUNKS = ((0, 4, True), (1, 4, True), (8, 2, False), (2, 4, True), (3, 4, True),
             (4, 4, False), (5, 4, False), (6, 0, False), (7, 0, False))


def _inproj_kernel(x_ref, g_ref, sh_ref, sc_ref, cos_ref, sin_ref, w_ref, o_ref):
    hb = _norm_mod(x_ref[...], g_ref[...], sh_ref[0], sc_ref[0]).astype(BF16)
    cos = cos_ref[...]
    sin = sin_ref[...]
    lane = lax.broadcasted_iota(jnp.int32, (IN_TM, LANES), 1)
    first = (lane & 63) < 32
    for c, (dest, n_rope, is_q) in enumerate(IN_CHUNKS):
        acc = jnp.dot(hb, w_ref[:, c * IN_CH:(c + 1) * IN_CH], preferred_element_type=F32)
        for s in range(IN_CH // LANES):
            y = acc[:, s * LANES:(s + 1) * LANES]
            if s < n_rope:
                partner = jnp.where(first, pltpu.roll(y, 96, 1), pltpu.roll(y, 32, 1))
                y = y * cos + partner * sin
                if is_q:
                    y = y * Q_SCALE
            col = dest * IN_CH + s * LANES
            o_ref[:, col:col + LANES] = y.astype(BF16)


def _in_proj(x, g, mod, w_bf, cos_t, sin_t):
    rows = x.shape[0]
    tm = IN_TM
    tab = lambda i: (jnp.where(i < NL // tm, i % (SEQ // tm), SEQ // tm), 0)
    return pl.pallas_call(
        _inproj_kernel,
        out_shape=jax.ShapeDtypeStruct((rows, IN_COLS), BF16),
        grid=(rows // tm,),
        in_specs=[
            pl.BlockSpec((tm, D), lambda i: (i, 0)),
            pl.BlockSpec((1, D), lambda i: (0, 0)),
            _mod_spec(tm, 0),
            _mod_spec(tm, 1),
            pl.BlockSpec((tm, LANES), tab),
            pl.BlockSpec((tm, LANES), tab),
            pl.BlockSpec((D, IN_COLS), lambda i: (0, 0)),
        ],
        out_specs=pl.BlockSpec((tm, IN_COLS), lambda i: (i, 0)),
        compiler_params=_cparams(("arbitrary",)),
        name="in_proj",
    )(x, g.reshape(1, D), mod, mod, cos_t, sin_t, w_bf)


ATT_TQ = 256
A_WIN = ATT_TQ + 2 * WINDOW


def _attn_a_kernel(sink_ref, q_ref, kc_ref, vc_ref, *rest, local):
    if local:
        kl_ref, vl_ref, o_ref = rest
    else:
        (o_ref,) = rest
    tq = ATT_TQ
    i = pl.program_id(1)
    for kv in range(A_KVH):
        ksl = slice(kv * HD, (kv + 1) * HD)
        qs = jnp.concatenate(
            [q_ref[:, (kv * A_GROUP + g) * HD:(kv * A_GROUP + g + 1) * HD] for g in range(A_GROUP)], axis=0)
        sink = jnp.concatenate(
            [jnp.full((tq, 1), sink_ref[kv * A_GROUP + g] * LOG2E, F32) for g in range(A_GROUP)], axis=0)
        s_ctx = _nt_dot(qs, kc_ref[:, ksl])
        m = jnp.maximum(jnp.max(s_ctx, axis=-1, keepdims=True), sink)
        if local:
            st = pl.multiple_of(jnp.clip(i * tq - WINDOW, 0, SEQ - A_WIN), WINDOW)
            s_loc = _nt_dot(qs, kl_ref[pl.ds(st, A_WIN), ksl])
            qpos = i * tq + (lax.broadcasted_iota(jnp.int32, (A_GROUP * tq, A_WIN), 0) & (tq - 1))
            kpos = st + lax.broadcasted_iota(jnp.int32, (A_GROUP * tq, A_WIN), 1)
            s_loc = jnp.where(jnp.abs(qpos - kpos) <= WINDOW, s_loc, NEG)
            m = jnp.maximum(m, jnp.max(s_loc, axis=-1, keepdims=True))
        p_ctx = jnp.exp2(s_ctx - m)
        den = jnp.sum(p_ctx, axis=-1, keepdims=True) + jnp.exp2(sink - m)
        o = jnp.dot(p_ctx.astype(BF16), vc_ref[:, ksl], preferred_element_type=F32)
        if local:
            p_loc = jnp.exp2(s_loc - m)
            den = den + jnp.sum(p_loc, axis=-1, keepdims=True)
            o = o + jnp.dot(p_loc.astype(BF16), vl_ref[pl.ds(st, A_WIN), ksl], preferred_element_type=F32)
        o = o * (1.0 / den)
        for g in range(A_GROUP):
            h = kv * A_GROUP + g
            o_ref[:, h * HD:(h + 1) * HD] = o[g * tq:(g + 1) * tq].astype(BF16)


def _attn_a(qkv, sink, local):
    tq = ATT_TQ
    nq = SEQ // tq if local else CTX // tq
    row0 = 0 if local else NL // tq
    per_b = nq
    ctx0 = NL // CTX
    in_specs = [
        pl.BlockSpec(memory_space=pltpu.SMEM),
        pl.BlockSpec((tq, 1024), lambda b, i: (row0 + b * per_b + i, COL_QA // 1024)),
        pl.BlockSpec((CTX, 256), lambda b, i: (ctx0 + b, COL_KA // 256)),
        pl.BlockSpec((CTX, 256), lambda b, i: (ctx0 + b, COL_VA // 256)),
    ]
    args = [sink.reshape(A_HEADS), qkv, qkv, qkv]
    if local:
        in_specs += [
            pl.BlockSpec((SEQ, 256), lambda b, i: (b, COL_KA // 256)),
            pl.BlockSpec((SEQ, 256), lambda b, i: (b, COL_VA // 256)),
        ]
        args += [qkv, qkv]
    rows = NL if local else NC
    return pl.pallas_call(
        functools.partial(_attn_a_kernel, local=local),
        out_shape=jax.ShapeDtypeStruct((rows, 1024), BF16),
        grid=(B, nq),
        in_specs=in_specs,
        out_specs=pl.BlockSpec((tq, 1024), lambda b, i: (b * per_b + i, 0)),
        compiler_params=_cparams(("arbitrary", "arbitrary")),
        name="attn_a_lat" if local else "attn_a_ctx",
    )(*args)


def _attn_b_kernel(lam_ref, g_ref, q_ref, kc_ref, vc_ref, *rest, with_lat, lam_init):
    if with_lat:
        kl_ref, vl_ref, o_ref = rest
    else:
        (o_ref,) = rest
    lv = lam_ref[...]
    lam = (jnp.exp(jnp.sum(lv[0:1] * lv[1:2], axis=-1, keepdims=True))
           - jnp.exp(jnp.sum(lv[2:3] * lv[3:4], axis=-1, keepdims=True)) + lam_init)
    for h in range(B_HEADS):
        ec, el, rden = [], [], []
        for mth in range(2):
            sl = slice((2 * h + mth) * HD, (2 * h + mth + 1) * HD)
            q = q_ref[:, sl]
            s_c = _nt_dot(q, kc_ref[:, sl])
            mx = jnp.max(s_c, axis=-1, keepdims=True)
            if with_lat:
                s_l = _nt_dot(q, kl_ref[:, sl])
                mx = jnp.maximum(mx, jnp.max(s_l, axis=-1, keepdims=True))
            e_c = jnp.exp2(s_c - mx)
            den = jnp.sum(e_c, axis=-1, keepdims=True)
            ec.append(e_c)
            if with_lat:
                e_l = jnp.exp2(s_l - mx)
                den = den + jnp.sum(e_l, axis=-1, keepdims=True)
                el.append(e_l)
            rden.append(1.0 / den)
        r1 = rden[0]
        r2 = lam * rden[1]
        vsl = slice(h * B_VDIM, (h + 1) * B_VDIM)
        o = jnp.dot((ec[0] * r1 - ec[1] * r2).astype(BF16), vc_ref[:, vsl], preferred_element_type=F32)
        if with_lat:
            o = o + jnp.dot((el[0] * r1 - el[1] * r2).astype(BF16), vl_ref[:, vsl],
                            preferred_element_type=F32)
        o = (_rms(o) * g_ref[...]) * (1.0 - lam_init)
        o_ref[:, vsl] = o.astype(BF16)


def _attn_b(qkv, lam_vecs, subln_g, lam_init, with_lat):
    tq = ATT_TQ
    nq = SEQ // tq if with_lat else CTX // tq
    row0 = 0 if with_lat else NL // tq
    ctx0 = NL // CTX
    in_specs = [
        pl.BlockSpec((4, HD), lambda b, i: (0, 0)),
        pl.BlockSpec((1, B_VDIM), lambda b, i: (0, 0)),
        pl.BlockSpec((tq, 1024), lambda b, i: (row0 + b * nq + i, COL_QB // 1024)),
        pl.BlockSpec((CTX, 1024), lambda b, i: (ctx0 + b, COL_KB // 1024)),
        pl.BlockSpec((CTX, 1024), lambda b, i: (ctx0 + b, COL_VB // 1024)),
    ]
    args = [lam_vecs, subln_g.reshape(1, B_VDIM), qkv, qkv, qkv]
    if with_lat:
        in_specs += [
            pl.BlockSpec((SEQ, 1024), lambda b, i: (b, COL_KB // 1024)),
            pl.BlockSpec((SEQ, 1024), lambda b, i: (b, COL_VB // 1024)),
        ]
        args += [qkv, qkv]
    rows = NL if with_lat else NC
    return pl.pallas_call(
        functools.partial(_attn_b_kernel, with_lat=with_lat, lam_init=lam_init),
        out_shape=jax.ShapeDtypeStruct((rows, 1024), BF16),
        grid=(B, nq),
        in_specs=in_specs,
        out_specs=pl.BlockSpec((tq, 1024), lambda b, i: (b * nq + i, 0)),
        compiler_params=_cparams(("arbitrary", "arbitrary")),
        name="attn_b_lat" if with_lat else "attn_b_ctx",
    )(*args)


OUT_TM = 512


def _outproj_kernel(x_ref, w_ref, g_ref, gate_ref, oal_ref, obl_ref, *rest, has_ctx):
    if has_ctx:
        oac_ref, obc_ref, o_ref, wb_scr = rest
    else:
        o_ref, wb_scr = rest
    i = pl.program_id(0)

    @pl.when(i == 0)
    def _():
        wb_scr[...] = w_ref[0].astype(BF16)

    def project(oa_ref, ob_ref):
        y = (jnp.dot(oa_ref[...], wb_scr[0:1024, :], preferred_element_type=F32)
             + jnp.dot(ob_ref[...], wb_scr[1024:2048, :], preferred_element_type=F32))
        o_ref[...] = x_ref[...] + gate_ref[0] * (_rms(y) * g_ref[...])

    if has_ctx:
        n_lat = NL // OUT_TM

        @pl.when(i < n_lat)
        def _():
            project(oal_ref, obl_ref)

        @pl.when(i >= n_lat)
        def _():
            project(oac_ref, obc_ref)
    else:
        project(oal_ref, obl_ref)


def _out_proj(x, attn_lat, attn_ctx, w_out, layer, g, mod, rows):
    tm = OUT_TM
    n_lat = NL // tm
    has_ctx = attn_ctx is not None
    lat_spec = pl.BlockSpec((tm, 1024), lambda i: (jnp.minimum(i, n_lat - 1), 0))
    ctx_spec = pl.BlockSpec((tm, 1024), lambda i: (jnp.maximum(i - n_lat, 0), 0))
    in_specs = [
        pl.BlockSpec((tm, D), lambda i: (i, 0)),
        pl.BlockSpec((1, D, D), lambda i: (layer, 0, 0), pipeline_mode=pl.Buffered(1)),
        pl.BlockSpec((1, D), lambda i: (0, 0)),
        _mod_spec(tm, 2),
        lat_spec, lat_spec,
    ]
    args = [x, w_out, g.reshape(1, D), mod, *attn_lat]
    if has_ctx:
        in_specs += [ctx_spec, ctx_spec]
        args += list(attn_ctx)
    return pl.pallas_call(
        functools.partial(_outproj_kernel, has_ctx=has_ctx),
        out_shape=jax.ShapeDtypeStruct((rows, D), F32),
        grid=(rows // tm,),
        in_specs=in_specs,
        out_specs=pl.BlockSpec((tm, D), lambda i: (i, 0)),
        scratch_shapes=[pltpu.VMEM((D, D), BF16)],
        compiler_params=_cparams(("arbitrary",)),
        name="out_proj",
    )(*args)


FFN_TM = 1024
FFN_TF = 256
FFN_HALF = 512


def _swiglu_step(hb, wg, wu, wd):
    gt = jnp.dot(hb, wg, preferred_element_type=F32)
    up = jnp.dot(hb, wu, preferred_element_type=F32)
    act = (gt * jax.nn.sigmoid(gt) * up).astype(BF16)
    return jnp.dot(act, wd, preferred_element_type=F32)


def _ffn_kernel(x_ref, gpre_ref, sh_ref, sc_ref, gate_ref, gpost_ref, wg_ref, wu_ref, wd_ref, o_ref,
                h_scr, wgb, wub, wdb):
    j = pl.program_id(1)

    @pl.when(j == 0)
    def _():
        h_scr[...] = _norm_mod(x_ref[...], gpre_ref[...], sh_ref[0], sc_ref[0]).astype(BF16)
        o_ref[...] = jnp.zeros_like(o_ref)

    wgb[...] = wg_ref[0].astype(BF16)
    wub[...] = wu_ref[0].astype(BF16)
    wdb[...] = wd_ref[0].astype(BF16)
    for r in range(FFN_TM // FFN_HALF):
        rows = slice(r * FFN_HALF, (r + 1) * FFN_HALF)
        o_ref[rows, :] += _swiglu_step(h_scr[rows, :], wgb[...], wub[...], wdb[...])

    @pl.when(j == pl.num_programs(1) - 1)
    def _():
        o_ref[...] = x_ref[...] + gate_ref[0] * (_rms(o_ref[...]) * gpost_ref[...])


def _ffn_dense(x, gpre, gpost, mod, wg, wu, wd, layer, rows):
    tm, tf = FFN_TM, FFN_TF
    return pl.pallas_call(
        _ffn_kernel,
        out_shape=jax.ShapeDtypeStruct((rows, D), F32),
        grid=(rows // tm, FF // tf),
        in_specs=[
            pl.BlockSpec((tm, D), lambda i, j: (i, 0), pipeline_mode=pl.Buffered(1)),
            pl.BlockSpec((1, D), lambda i, j: (0, 0)),
            _mod_spec(tm, 3, 2),
            _mod_spec(tm, 4, 2),
            _mod_spec(tm, 5, 2),
            pl.BlockSpec((1, D), lambda i, j: (0, 0)),
            pl.BlockSpec((1, D, tf), lambda i, j: (layer, 0, j)),
            pl.BlockSpec((1, D, tf), lambda i, j: (layer, 0, j)),
            pl.BlockSpec((1, tf, D), lambda i, j: (layer, j, 0)),
        ],
        out_specs=pl.BlockSpec((tm, D), lambda i, j: (i, 0)),
        scratch_shapes=[pltpu.VMEM((tm, D), BF16), pltpu.VMEM((D, tf), BF16), pltpu.VMEM((D, tf), BF16),
                        pltpu.VMEM((tf, D), BF16)],
        compiler_params=_cparams(("arbitrary", "arbitrary")),
        name="ffn_dense",
    )(x, gpre.reshape(1, D), mod, mod, mod, gpost.reshape(1, D), wg, wu, wd)


RT_TM = 512
MOE_SUB = 256
MOE_RS = 2560
MOE_TF = 256


def _router_kernel(x_ref, gpre_ref, sh_ref, sc_ref, wr_ref, h_ref, ri_ref, rw_ref):
    h = _norm_mod(x_ref[...], gpre_ref[...], sh_ref[0], sc_ref[0])
    h_ref[...] = h
    logits = jnp.dot(h, wr_ref[...], precision=HIGHEST, preferred_element_type=F32)
    lane = lax.broadcasted_iota(jnp.int32, logits.shape, 1)
    logits = jnp.where(lane < NE, logits, -jnp.inf)
    m1 = jnp.max(logits, axis=-1, keepdims=True)
    i1 = jnp.min(jnp.where(logits == m1, lane, LANES), axis=-1, keepdims=True)
    rest = jnp.where(lane == i1, -jnp.inf, logits)
    m2 = jnp.max(rest, axis=-1, keepdims=True)
    i2 = jnp.min(jnp.where(rest == m2, lane, LANES), axis=-1, keepdims=True)
    e2 = jnp.exp(m2 - m1)
    den = 1.0 + e2
    ri_ref[...] = jnp.where(lane == 0, i1, jnp.where(lane == 1, i2, 0))
    rw_ref[...] = jnp.where(lane == 0, 1.0 / den, jnp.where(lane == 1, e2 / den, 0.0))


def _router(x, gpre, mod, wr_pad, rows):
    tm = RT_TM
    return pl.pallas_call(
        _router_kernel,
        out_shape=(jax.ShapeDtypeStruct((rows, D), F32),
                   jax.ShapeDtypeStruct((rows, LANES), jnp.int32),
                   jax.ShapeDtypeStruct((rows, LANES), F32)),
        grid=(rows // tm,),
        in_specs=[
            pl.BlockSpec((tm, D), lambda i: (i, 0)),
            pl.BlockSpec((1, D), lambda i: (0, 0)),
            _mod_spec(tm, 3),
            _mod_spec(tm, 4),
            pl.BlockSpec((D, LANES), lambda i: (0, 0)),
        ],
        out_specs=(pl.BlockSpec((tm, D), lambda i: (i, 0)),
                   pl.BlockSpec((tm, LANES), lambda i: (i, 0)),
                   pl.BlockSpec((tm, LANES), lambda i: (i, 0))),
        compiler_params=_cparams(("arbitrary",)),
        name="moe_router",
    )(x, gpre.reshape(1, D), mod, mod, wr_pad)


def _moe_kernel(te_ref, nsub_ref, src_ref, dst_ref, h_hbm, wg_ref, wu_ref, wd_ref, y_hbm,
                x_scr, stage, acc_scr, wgb, wub, wdb, sem_in, sem_out):
    s = pl.program_id(0)
    j = pl.program_id(1)
    sub = MOE_SUB
    nsub = nsub_ref[s]
    active = nsub > 0

    def rows_of(c):
        return pl.ds(pl.multiple_of(c * sub, sub), sub)

    @pl.when(jnp.logical_and(s == 0, j == 0))
    def _():
        stage[0] = jnp.zeros((sub, D), F32)
        dump = pltpu.make_async_copy(stage.at[0], y_hbm.at[pl.ds(y_hbm.shape[0] - sub, sub)], sem_out)
        dump.start()
        dump.wait()

    @pl.when(jnp.logical_and(active, j == 0))
    def _():
        def issue(c, slot):
            def one(r, carry):
                pltpu.make_async_copy(h_hbm.at[pl.ds(src_ref[0, 0, c * sub + r], 1)],
                                      stage.at[slot, pl.ds(r, 1)], sem_in.at[slot]).start()
                return carry
            lax.fori_loop(0, sub, one, 0, unroll=8)

        issue(0, 0)

        def chunk(c, carry):
            slot = c & 1

            @pl.when(c + 1 < nsub)
            def _():
                issue(c + 1, 1 - slot)

            pltpu.make_async_copy(h_hbm.at[pl.ds(0, sub)], stage.at[slot], sem_in.at[slot]).wait()
            x_scr[rows_of(c)] = stage[slot].astype(BF16)
            acc_scr[rows_of(c)] = jnp.zeros((sub, D), F32)
            return carry
        lax.fori_loop(0, nsub, chunk, 0)

    @pl.when(active)
    def _():
        wgb[...] = wg_ref[0, 0].astype(BF16)
        wub[...] = wu_ref[0, 0].astype(BF16)
        wdb[...] = wd_ref[0, 0].astype(BF16)

        def pair(p, carry):
            rows = pl.ds(pl.multiple_of(p * (2 * sub), 2 * sub), 2 * sub)
            acc_scr[rows] += _swiglu_step(x_scr[rows], wgb[...], wub[...], wdb[...])
            return carry
        lax.fori_loop(0, nsub >> 1, pair, 0)

        @pl.when((nsub & 1) == 1)
        def _():
            rows = rows_of(nsub - 1)
            acc_scr[rows] += _swiglu_step(x_scr[rows], wgb[...], wub[...], wdb[...])

    @pl.when(jnp.logical_and(active, j == pl.num_programs(1) - 1))
    def _():
        def issue(c, carry):
            def one(r, carry2):
                row = c * sub + r
                pltpu.make_async_copy(acc_scr.at[pl.ds(row, 1)], y_hbm.at[pl.ds(dst_ref[0, 0, row], 1)],
                                      sem_out).start()
                return carry2
            lax.fori_loop(0, sub, one, 0, unroll=8)
            return carry
        lax.fori_loop(0, nsub, issue, 0)

        def drain(c, carry):
            pltpu.make_async_copy(acc_scr.at[pl.ds(0, sub)], y_hbm.at[pl.ds(0, sub)], sem_out).wait()
            return carry
        lax.fori_loop(0, nsub, drain, 0)


def _moe_plan(ri, rows):
    sub, rs = MOE_SUB, MOE_RS
    n_asg = 2 * rows
    ns_max = (n_asg + NE * (sub - 1)) // rs + NE
    e_flat = jnp.concatenate([ri[:, 0], ri[:, 1]])
    _, order = lax.sort((e_flat, jnp.arange(n_asg, dtype=jnp.int32)), num_keys=1)
    counts = jnp.sum((e_flat[:, None] == jnp.arange(NE, dtype=jnp.int32)[None, :]).astype(jnp.int32), axis=0)
    ustart = jnp.cumsum(counts) - counts
    psz = ((counts + sub - 1) // sub) * sub
    nst = (psz + rs - 1) // rs
    st_end = jnp.cumsum(nst)
    total = st_end[NE - 1]
    s_idx = jnp.arange(ns_max, dtype=jnp.int32)
    valid = s_idx < total
    s_eff = jnp.minimum(s_idx, total - 1)
    e_s = jnp.sum((s_eff[:, None] >= st_end[None, :]).astype(jnp.int32), axis=1)
    k_s = s_eff - (st_end - nst)[e_s]
    nsub = jnp.where(valid, jnp.clip(psz[e_s] - k_s * rs, 0, rs) // sub, 0)
    lane = jnp.arange(rs, dtype=jnp.int32)[None, :]
    n_real = jnp.where(valid, jnp.clip(counts[e_s] - k_s * rs, 0, rs), 0)
    row_valid = lane < n_real[:, None]
    base = ustart[e_s] + k_s * rs
    order_pad = jnp.concatenate([order, jnp.zeros((rs,), jnp.int32)])
    asg = jnp.stack([lax.dynamic_slice(order_pad, (base[s],), (rs,)) for s in range(ns_max)])
    src = jnp.where(row_valid, asg % rows, 0)
    dst = jnp.where(row_valid, asg, n_asg + lane % sub)
    return (e_s.astype(jnp.int32), nsub.astype(jnp.int32),
            src.astype(jnp.int32).reshape(ns_max, 1, rs), dst.astype(jnp.int32).reshape(ns_max, 1, rs))


def _moe_experts(h, plan, wg, wu, wd, layer, rows):
    sub, rs, tf = MOE_SUB, MOE_RS, MOE_TF
    tile_e, nsub, src, dst = plan
    ns_max = src.shape[0]
    nj = FF // tf
    jmap = lambda j, ns, s: jnp.where(ns[s] > 0, j, nj - 1)
    grid_spec = pltpu.PrefetchScalarGridSpec(
        num_scalar_prefetch=2,
        grid=(ns_max, nj),
        in_specs=[
            pl.BlockSpec((1, 1, rs), lambda s, j, te, ns: (s, 0, 0), memory_space=pltpu.SMEM),
            pl.BlockSpec((1, 1, rs), lambda s, j, te, ns: (s, 0, 0), memory_space=pltpu.SMEM),
            pl.BlockSpec(memory_space=pl.ANY),
            pl.BlockSpec((1, 1, D, tf), lambda s, j, te, ns: (layer, te[s], 0, jmap(j, ns, s))),
            pl.BlockSpec((1, 1, D, tf), lambda s, j, te, ns: (layer, te[s], 0, jmap(j, ns, s))),
            pl.BlockSpec((1, 1, tf, D), lambda s, j, te, ns: (layer, te[s], jmap(j, ns, s), 0)),
        ],
        out_specs=pl.BlockSpec(memory_space=pl.ANY),
        scratch_shapes=[
            pltpu.VMEM((rs, D), BF16),
            pltpu.VMEM((2, sub, D), F32),
            pltpu.VMEM((rs, D), F32),
            pltpu.VMEM((D, tf), BF16),
            pltpu.VMEM((D, tf), BF16),
            pltpu.VMEM((tf, D), BF16),
            pltpu.SemaphoreType.DMA((2,)),
            pltpu.SemaphoreType.DMA(()),
        ],
    )
    return pl.pallas_call(
        _moe_kernel,
        out_shape=jax.ShapeDtypeStruct((2 * rows + sub, D), F32),
        grid_spec=grid_spec,
        compiler_params=_cparams(("arbitrary", "arbitrary")),
        name="moe_experts",
    )(tile_e, nsub, src, dst, h, wg, wu, wd)


POST_TM = 512


def _moe_post_kernel(x_ref, y0_ref, y1_ref, rw_ref, gpost_ref, gate_ref, o_ref):
    w = rw_ref[...]
    f = w[:, 0:1] * y0_ref[...] + w[:, 1:2] * y1_ref[...]
    o_ref[...] = x_ref[...] + gate_ref[0] * (_rms(f) * gpost_ref[...])


def _moe_post(x, y2, rw, gpost, mod, rows):
    tm = POST_TM
    off = rows // tm
    return pl.pallas_call(
        _moe_post_kernel,
        out_shape=jax.ShapeDtypeStruct((rows, D), F32),
        grid=(rows // tm,),
        in_specs=[
            pl.BlockSpec((tm, D), lambda i: (i, 0)),
            pl.BlockSpec((tm, D), lambda i: (i, 0)),
            pl.BlockSpec((tm, D), lambda i: (off + i, 0)),
            pl.BlockSpec((tm, LANES), lambda i: (i, 0)),
            pl.BlockSpec((1, D), lambda i: (0, 0)),
            _mod_spec(tm, 5),
        ],
        out_specs=pl.BlockSpec((tm, D), lambda i: (i, 0)),
        compiler_params=_cparams(("arbitrary",)),
        name="moe_post",
    )(x, y2, y2, rw, gpost.reshape(1, D), mod)


def _rope_tables():
    rows = SEQ // GRID_W
    row = jnp.repeat(jnp.arange(rows), GRID_W).astype(F32)
    col = jnp.tile(jnp.arange(GRID_W), rows).astype(F32)
    inv = ROPE_BASE ** (-jnp.arange(ROPE_PAIRS, dtype=F32) / ROPE_PAIRS)
    ang_r = row[:, None] * inv[None, :]
    ang_c = col[:, None] * inv[None, :]
    cr, sr, cc, sc = jnp.cos(ang_r), jnp.sin(ang_r), jnp.cos(ang_c), jnp.sin(ang_c)
    cos_t = jnp.concatenate([cr, cr, cc, cc], axis=-1)
    sin_t = jnp.concatenate([-sr, sr, -sc, sc], axis=-1)
    cos_t = jnp.concatenate([cos_t, jnp.ones((IN_TM, LANES), F32)], axis=0)
    sin_t = jnp.concatenate([sin_t, jnp.zeros((IN_TM, LANES), F32)], axis=0)
    return cos_t, sin_t


def kernel(x, c, ctx, c_ctx, w_mod, b_mod, g_attn_pre, g_attn_post, g_ffn_pre, g_ffn_post, w_in, w_out,
           sink_logit, lambda_q1, lambda_k1, lambda_q2, lambda_k2, subln_g, ffn_w_gate, ffn_w_up,
           ffn_w_down, moe_router, moe_w_gate, moe_w_up, moe_w_down):
    xs = jnp.concatenate([x.reshape(NL, D), ctx.reshape(NC, D)], axis=0)
    cin = jnp.concatenate([c, c_ctx[None, :], jnp.zeros((8 - B - 1, D), F32)], axis=0)
    mod_all = _modulation(cin, w_mod, b_mod)
    cos_t, sin_t = _rope_tables()

    for l in range(DEPTH):
        last = l == DEPTH - 1
        rows = NL if last else NR
        mod = mod_all[l].reshape(8, 1, N_MOD * D)
        lam_init = 0.8 - 0.6 * math.exp(-0.3 * l)
        lam_vecs = jnp.stack([lambda_q1[l], lambda_k1[l], lambda_q2[l], lambda_k2[l]])

        qkv = _in_proj(xs, g_attn_pre[l], mod, w_in[l].astype(BF16), cos_t, sin_t)
        attn_lat = (_attn_a(qkv, sink_logit[l], local=True),
                    _attn_b(qkv, lam_vecs, subln_g[l], lam_init, with_lat=True))
        attn_ctx = None if last else (_attn_a(qkv, sink_logit[l], local=False),
                                      _attn_b(qkv, lam_vecs, subln_g[l], lam_init, with_lat=False))
        xs = _out_proj(xs, attn_lat, attn_ctx, w_out, l, g_attn_post[l], mod, rows)

        i = l // 2
        if l % 2 == 0:
            xs = _ffn_dense(xs, g_ffn_pre[l], g_ffn_post[l], mod, ffn_w_gate, ffn_w_up, ffn_w_down, i, rows)
        else:
            wr_pad = jnp.pad(moe_router[i], ((0, 0), (0, LANES - NE)))
            h, ri, rw = _router(xs, g_ffn_pre[l], mod, wr_pad, rows)
            plan = _moe_plan(ri, rows)
            y2 = _moe_experts(h, plan, moe_w_gate, moe_w_up, moe_w_down, i, rows)
            xs = _moe_post(xs, y2, rw, g_ffn_post[l], mod, rows)
    return xs.reshape(B, SEQ, D)
```

```python
import functools
import math

import jax
import jax.numpy as jnp
from jax import lax
from jax.experimental import pallas as pl
from jax.experimental.pallas import tpu as pltpu

D = 2048
B = 4
SEQ = 2048
DEPTH = 4
GRID_W = 64
CTX = 256
HD = 128
WINDOW = 128
A_HEADS = 8
A_KVH = 2
A_GROUP = 4
B_HEADS = 4
B_VDIM = 256
FF = 5632
NE = 8
N_MOD = 6
ROPE_BASE = 10000.0
ROPE_PAIRS = 32
ATTN_SCALE = HD ** -0.5
EPS = 1e-6
NEG = -1e30

NL = B * SEQ
NC = B * CTX
NR = NL + NC
IN_COLS = 4608
COL_QA, COL_QB, COL_KB, COL_VB, COL_KA, COL_VA = 0, 1024, 2048, 3072, 4096, 4352

LANES = 128
VMEM_LIMIT = 56 * 1024 * 1024

F32 = jnp.float32
BF16 = jnp.bfloat16


def _cparams(sem):
    return pltpu.CompilerParams(dimension_semantics=sem, vmem_limit_bytes=VMEM_LIMIT)


def _nt_dot(a, b):
    return lax.dot_general(a, b, (((1,), (1,)), ((), ())), preferred_element_type=F32)


def _rms(x):
    return x * lax.rsqrt(jnp.mean(x * x, axis=-1, keepdims=True) + EPS)


def _norm_mod(x, g, sh, sc):
    return (_rms(x) * g) * (1.0 + sc) + sh


def _mod_row(i, tm):
    return jnp.where(i < NL // tm, (i * tm) // SEQ, B)


def _mod_spec(tm, chunk, grid_rank=1):
    if grid_rank == 1:
        return pl.BlockSpec((1, 1, D), lambda i: (_mod_row(i, tm), 0, chunk))
    return pl.BlockSpec((1, 1, D), lambda i, j: (_mod_row(i, tm), 0, chunk))


MOD_TN = 1024


def _split_bf16(v):
    hi = v.astype(BF16)
    return hi, (v - hi.astype(F32)).astype(BF16)


def _mod_kernel(c_ref, w_ref, b_ref, o_ref):
    c = c_ref[...]
    a_hi, a_lo = _split_bf16(c * jax.nn.sigmoid(c))
    w_hi, w_lo = _split_bf16(w_ref[0])
    both = jnp.dot(jnp.concatenate([a_hi, a_lo], axis=0), w_hi, preferred_element_type=F32)
    o_ref[0] = both[0:8] + both[8:16] + jnp.dot(a_hi, w_lo, preferred_element_type=F32) + b_ref[0]


def _modulation(cin, w_mod, b_mod):
    return pl.pallas_call(
        _mod_kernel,
        out_shape=jax.ShapeDtypeStruct((DEPTH, 8, N_MOD * D), F32),
        grid=(DEPTH, N_MOD * D // MOD_TN),
        in_specs=[
            pl.BlockSpec((8, D), lambda l, n: (0, 0)),
            pl.BlockSpec((1, D, MOD_TN), lambda l, n: (l, 0, n)),
            pl.BlockSpec((1, 1, MOD_TN), lambda l, n: (l, 0, n)),
        ],
        out_specs=pl.BlockSpec((1, 8, MOD_TN), lambda l, n: (l, 0, n)),
        compiler_params=_cparams(("arbitrary", "arbitrary")),
        name="modulation",
    )(cin, w_mod, b_mod.reshape(DEPTH, 1, N_MOD * D))


IN_TM = 256
IN_CH = 512
LOG2E = math.log2(math.e)
Q_SCALE = ATTN_SCALE * LOG2E
IN_CHUNKS = ((0, 4, True), (1, 4, True), (8, 2, False), (2, 4, True), (3, 4, True),
             (4, 4, False), (5, 4, False), (6, 0, False), (7, 0, False))


def _inproj_kernel(x_ref, g_ref, sh_ref, sc_ref, cos_ref, sin_ref, w_ref, o_ref):
    hb = _norm_mod(x_ref[...], g_ref[...], sh_ref[0], sc_ref[0]).astype(BF16)
    cos = cos_ref[...]
    sin = sin_ref[...]
    lane = lax.broadcasted_iota(jnp.int32, (IN_TM, LANES), 1)
    first = (lane & 63) < 32
    for c, (dest, n_rope, is_q) in enumerate(IN_CHUNKS):
        acc = jnp.dot(hb, w_ref[:, c * IN_CH:(c + 1) * IN_CH], preferred_element_type=F32)
        for s in range(IN_CH // LANES):
            y = acc[:, s * LANES:(s + 1) * LANES]
            if s < n_rope:
                partner = jnp.where(first, pltpu.roll(y, 96, 1), pltpu.roll(y, 32, 1))
                y = y * cos + partner * sin
                if is_q:
                    y = y * Q_SCALE
            col = dest * IN_CH + s * LANES
            o_ref[:, col:col + LANES] = y.astype(BF16)


def _in_proj(x, g, mod, w_bf, cos_t, sin_t):
    rows = x.shape[0]
    tm = IN_TM
    tab = lambda i: (jnp.where(i < NL // tm, i % (SEQ // tm), SEQ // tm), 0)
    return pl.pallas_call(
        _inproj_kernel,
        out_shape=jax.ShapeDtypeStruct((rows, IN_COLS), BF16),
        grid=(rows // tm,),
        in_specs=[
            pl.BlockSpec((tm, D), lambda i: (i, 0)),
            pl.BlockSpec((1, D), lambda i: (0, 0)),
            _mod_spec(tm, 0),
            _mod_spec(tm, 1),
            pl.BlockSpec((tm, LANES), tab),
            pl.BlockSpec((tm, LANES), tab),
            pl.BlockSpec((D, IN_COLS), lambda i: (0, 0)),
        ],
        out_specs=pl.BlockSpec((tm, IN_COLS), lambda i: (i, 0)),
        compiler_params=_cparams(("arbitrary",)),
        name="in_proj",
    )(x, g.reshape(1, D), mod, mod, cos_t, sin_t, w_bf)


ATT_TQ_LAT = 256
ATT_TQ_CTX = 256


def _attn_a_body(sink_ref, q_ref, kc_ref, vc_ref, kl_ref, vl_ref, o_ref, local, kv_heads):
    tq = q_ref.shape[0]
    a_win = tq + 2 * WINDOW
    i = pl.program_id(1)
    for kv in kv_heads:
        ksl = slice(kv * HD, (kv + 1) * HD)
        qs = jnp.concatenate(
            [q_ref[:, (kv * A_GROUP + g) * HD:(kv * A_GROUP + g + 1) * HD] for g in range(A_GROUP)], axis=0)
        sink = jnp.concatenate(
            [jnp.full((tq, 1), sink_ref[kv * A_GROUP + g] * LOG2E, F32) for g in range(A_GROUP)], axis=0)
        s_ctx = _nt_dot(qs, kc_ref[:, ksl])
        m = jnp.maximum(jnp.max(s_ctx, axis=-1, keepdims=True), sink)
        if local:
            st = pl.multiple_of(jnp.clip(i * tq - WINDOW, 0, SEQ - a_win), WINDOW)
            s_loc = _nt_dot(qs, kl_ref[pl.ds(st, a_win), ksl])
            qpos = i * tq + (lax.broadcasted_iota(jnp.int32, (A_GROUP * tq, a_win), 0) & (tq - 1))
            kpos = st + lax.broadcasted_iota(jnp.int32, (A_GROUP * tq, a_win), 1)
            s_loc = jnp.where(jnp.abs(qpos - kpos) <= WINDOW, s_loc, NEG)
            m = jnp.maximum(m, jnp.max(s_loc, axis=-1, keepdims=True))
        p_ctx = jnp.exp2(s_ctx - m)
        den = jnp.sum(p_ctx, axis=-1, keepdims=True) + jnp.exp2(sink - m)
        o = jnp.dot(p_ctx.astype(BF16), vc_ref[:, ksl], preferred_element_type=F32)
        if local:
            p_loc = jnp.exp2(s_loc - m)
            den = den + jnp.sum(p_loc, axis=-1, keepdims=True)
            o = o + jnp.dot(p_loc.astype(BF16), vl_ref[pl.ds(st, a_win), ksl], preferred_element_type=F32)
        o = o * (1.0 / den)
        for g in range(A_GROUP):
            h = kv * A_GROUP + g
            o_ref[:, h * HD:(h + 1) * HD] = o[g * tq:(g + 1) * tq].astype(BF16)


def _attn_b_body(lam_ref, g_ref, q_ref, kc_ref, vc_ref, kl_ref, vl_ref, o_ref, with_lat, lam_init, heads):
    lv = lam_ref[...]
    lam = (jnp.exp(jnp.sum(lv[0:1] * lv[1:2], axis=-1, keepdims=True))
           - jnp.exp(jnp.sum(lv[2:3] * lv[3:4], axis=-1, keepdims=True)) + lam_init)
    for h in heads:
        ec, el, rden = [], [], []
        for mth in range(2):
            sl = slice((2 * h + mth) * HD, (2 * h + mth + 1) * HD)
            q = q_ref[:, sl]
            s_c = _nt_dot(q, kc_ref[:, sl])
            mx = jnp.max(s_c, axis=-1, keepdims=True)
            if with_lat:
                s_l = _nt_dot(q, kl_ref[:, sl])
                mx = jnp.maximum(mx, jnp.max(s_l, axis=-1, keepdims=True))
            e_c = jnp.exp2(s_c - mx)
            den = jnp.sum(e_c, axis=-1, keepdims=True)
            ec.append(e_c)
            if with_lat:
                e_l = jnp.exp2(s_l - mx)
                den = den + jnp.sum(e_l, axis=-1, keepdims=True)
                el.append(e_l)
            rden.append(1.0 / den)
        r1 = rden[0]
        r2 = lam * rden[1]
        vsl = slice(h * B_VDIM, (h + 1) * B_VDIM)
        o = jnp.dot((ec[0] * r1 - ec[1] * r2).astype(BF16), vc_ref[:, vsl], preferred_element_type=F32)
        if with_lat:
            o = o + jnp.dot((el[0] * r1 - el[1] * r2).astype(BF16), vl_ref[:, vsl],
                            preferred_element_type=F32)
        o = (_rms(o) * g_ref[...]) * (1.0 - lam_init)
        o_ref[:, vsl] = o.astype(BF16)


def _mixers_kernel(sink_ref, lam_ref, g_ref, qa_ref, kac_ref, vac_ref, qb_ref, kbc_ref, vbc_ref, *rest,
                   lat, lam_init):
    if lat:
        kal_ref, val_ref, kbl_ref, vbl_ref, oa_ref, ob_ref = rest
    else:
        oa_ref, ob_ref = rest
        kal_ref = val_ref = kbl_ref = vbl_ref = None
    mixer_a = functools.partial(_attn_a_body, sink_ref, qa_ref, kac_ref, vac_ref, kal_ref, val_ref, oa_ref, lat)
    mixer_b = functools.partial(_attn_b_body, lam_ref, g_ref, qb_ref, kbc_ref, vbc_ref, kbl_ref, vbl_ref, ob_ref,
                                lat, lam_init)
    mixer_a(range(A_KVH))
    mixer_b(range(B_HEADS))


def _mixers(qkv, sink, lam_vecs, subln_g, lam_init, lat):
    tq = ATT_TQ_LAT if lat else ATT_TQ_CTX
    nq = SEQ // tq if lat else CTX // tq
    row0 = 0 if lat else NL // tq
    ctx0 = NL // CTX
    qrow = lambda b, i: row0 + b * nq + i
    in_specs = [
        pl.BlockSpec(memory_space=pltpu.SMEM),
        pl.BlockSpec((4, HD), lambda b, i: (0, 0)),
        pl.BlockSpec((1, B_VDIM), lambda b, i: (0, 0)),
        pl.BlockSpec((tq, 1024), lambda b, i: (qrow(b, i), COL_QA // 1024)),
        pl.BlockSpec((CTX, 256), lambda b, i: (ctx0 + b, COL_KA // 256)),
        pl.BlockSpec((CTX, 256), lambda b, i: (ctx0 + b, COL_VA // 256)),
        pl.BlockSpec((tq, 1024), lambda b, i: (qrow(b, i), COL_QB // 1024)),
        pl.BlockSpec((CTX, 1024), lambda b, i: (ctx0 + b, COL_KB // 1024)),
        pl.BlockSpec((CTX, 1024), lambda b, i: (ctx0 + b, COL_VB // 1024)),
    ]
    args = [sink.reshape(A_HEADS), lam_vecs, subln_g.reshape(1, B_VDIM)] + [qkv] * 6
    if lat:
        in_specs += [
            pl.BlockSpec((SEQ, 256), lambda b, i: (b, COL_KA // 256)),
            pl.BlockSpec((SEQ, 256), lambda b, i: (b, COL_VA // 256)),
            pl.BlockSpec((SEQ, 1024), lambda b, i: (b, COL_KB // 1024)),
            pl.BlockSpec((SEQ, 1024), lambda b, i: (b, COL_VB // 1024)),
        ]
        args += [qkv] * 4
    rows = NL if lat else NC
    out = jax.ShapeDtypeStruct((rows, 1024), BF16)
    out_spec = pl.BlockSpec((tq, 1024), lambda b, i: (b * nq + i, 0))
    return pl.pallas_call(
        functools.partial(_mixers_kernel, lat=lat, lam_init=lam_init),
        out_shape=(out, out),
        grid=(B, nq),
        in_specs=in_specs,
        out_specs=(out_spec, out_spec),
        compiler_params=_cparams(("arbitrary", "arbitrary")),
        name="mixers_lat" if lat else "mixers_ctx",
    )(*args)


OUT_TM = 512


def _outproj_kernel(x_ref, w_ref, g_ref, gate_ref, oal_ref, obl_ref, *rest, has_ctx):
    if has_ctx:
        oac_ref, obc_ref, o_ref, wb_scr = rest
    else:
        o_ref, wb_scr = rest
    i = pl.program_id(0)

    @pl.when(i == 0)
    def _():
        wb_scr[...] = w_ref[0].astype(BF16)

    def project(oa_ref, ob_ref):
        y = (jnp.dot(oa_ref[...], wb_scr[0:1024, :], preferred_element_type=F32)
             + jnp.dot(ob_ref[...], wb_scr[1024:2048, :], preferred_element_type=F32))
        o_ref[...] = x_ref[...] + gate_ref[0] * (_rms(y) * g_ref[...])

    if has_ctx:
        n_lat = NL // OUT_TM

        @pl.when(i < n_lat)
        def _():
            project(oal_ref, obl_ref)

        @pl.when(i >= n_lat)
        def _():
            project(oac_ref, obc_ref)
    else:
        project(oal_ref, obl_ref)


def _out_proj(x, attn_lat, attn_ctx, w_out, layer, g, mod, rows):
    tm = OUT_TM
    n_lat = NL // tm
    has_ctx = attn_ctx is not None
    lat_spec = pl.BlockSpec((tm, 1024), lambda i: (jnp.minimum(i, n_lat - 1), 0))
    ctx_spec = pl.BlockSpec((tm, 1024), lambda i: (jnp.maximum(i - n_lat, 0), 0))
    in_specs = [
        pl.BlockSpec((tm, D), lambda i: (i, 0)),
        pl.BlockSpec((1, D, D), lambda i: (layer, 0, 0), pipeline_mode=pl.Buffered(1)),
        pl.BlockSpec((1, D), lambda i: (0, 0)),
        _mod_spec(tm, 2),
        lat_spec, lat_spec,
    ]
    args = [x, w_out, g.reshape(1, D), mod, *attn_lat]
    if has_ctx:
        in_specs += [ctx_spec, ctx_spec]
        args += list(attn_ctx)
    return pl.pallas_call(
        functools.partial(_outproj_kernel, has_ctx=has_ctx),
        out_shape=jax.ShapeDtypeStruct((rows, D), F32),
        grid=(rows // tm,),
        in_specs=in_specs,
        out_specs=pl.BlockSpec((tm, D), lambda i: (i, 0)),
        scratch_shapes=[pltpu.VMEM((D, D), BF16)],
        compiler_params=_cparams(("arbitrary",)),
        name="out_proj",
    )(*args)


FFN_TM = 1024
FFN_TF = 256
FFN_HALF = 512


def _swiglu_step(hb, wg, wu, wd):
    gt = jnp.dot(hb, wg, preferred_element_type=F32)
    up = jnp.dot(hb, wu, preferred_element_type=F32)
    act = (gt * jax.nn.sigmoid(gt) * up).astype(BF16)
    return jnp.dot(act, wd, preferred_element_type=F32)


def _ffn_kernel(x_ref, gpre_ref, sh_ref, sc_ref, gate_ref, gpost_ref, wg_ref, wu_ref, wd_ref, o_ref,
                h_scr, wgb, wub, wdb):
    j = pl.program_id(1)

    @pl.when(j == 0)
    def _():
        h_scr[...] = _norm_mod(x_ref[...], gpre_ref[...], sh_ref[0], sc_ref[0]).astype(BF16)
        o_ref[...] = jnp.zeros_like(o_ref)

    wgb[...] = wg_ref[0].astype(BF16)
    wub[...] = wu_ref[0].astype(BF16)
    wdb[...] = wd_ref[0].astype(BF16)
    for r in range(FFN_TM // FFN_HALF):
        rows = slice(r * FFN_HALF, (r + 1) * FFN_HALF)
        o_ref[rows, :] += _swiglu_step(h_scr[rows, :], wgb[...], wub[...], wdb[...])

    @pl.when(j == pl.num_programs(1) - 1)
    def _():
        o_ref[...] = x_ref[...] + gate_ref[0] * (_rms(o_ref[...]) * gpost_ref[...])


def _ffn_dense(x, gpre, gpost, mod, wg, wu, wd, layer, rows):
    tm, tf = FFN_TM, FFN_TF
    return pl.pallas_call(
        _ffn_kernel,
        out_shape=jax.ShapeDtypeStruct((rows, D), F32),
        grid=(rows // tm, FF // tf),
        in_specs=[
            pl.BlockSpec((tm, D), lambda i, j: (i, 0), pipeline_mode=pl.Buffered(1)),
            pl.BlockSpec((1, D), lambda i, j: (0, 0)),
            _mod_spec(tm, 3, 2),
            _mod_spec(tm, 4, 2),
            _mod_spec(tm, 5, 2),
            pl.BlockSpec((1, D), lambda i, j: (0, 0)),
            pl.BlockSpec((1, D, tf), lambda i, j: (layer, 0, j)),
            pl.BlockSpec((1, D, tf), lambda i, j: (layer, 0, j)),
            pl.BlockSpec((1, tf, D), lambda i, j: (layer, j, 0)),
        ],
        out_specs=pl.BlockSpec((tm, D), lambda i, j: (i, 0)),
        scratch_shapes=[pltpu.VMEM((tm, D), BF16), pltpu.VMEM((D, tf), BF16), pltpu.VMEM((D, tf), BF16),
                        pltpu.VMEM((tf, D), BF16)],
        compiler_params=_cparams(("arbitrary", "arbitrary")),
        name="ffn_dense",
    )(x, gpre.reshape(1, D), mod, mod, mod, gpost.reshape(1, D), wg, wu, wd)


RT_TM = 512
MOE_SUB = 256
MOE_RS = 2560
MOE_TF = 256


def _router_kernel(x_ref, gpre_ref, sh_ref, sc_ref, wr_ref, h_ref, ri_ref, rw_ref):
    h = _norm_mod(x_ref[...], gpre_ref[...], sh_ref[0], sc_ref[0])
    h_ref[...] = h
    h_hi, h_lo = _split_bf16(h)
    w_hi, w_lo = _split_bf16(wr_ref[...])
    logits = (jnp.dot(h_hi, w_hi, preferred_element_type=F32) + jnp.dot(h_lo, w_hi, preferred_element_type=F32)
              + jnp.dot(h_hi, w_lo, preferred_element_type=F32))
    lane = lax.broadcasted_iota(jnp.int32, logits.shape, 1)
    logits = jnp.where(lane < NE, logits, -jnp.inf)
    m1 = jnp.max(logits, axis=-1, keepdims=True)
    i1 = jnp.min(jnp.where(logits == m1, lane, LANES), axis=-1, keepdims=True)
    rest = jnp.where(lane == i1, -jnp.inf, logits)
    m2 = jnp.max(rest, axis=-1, keepdims=True)
    i2 = jnp.min(jnp.where(rest == m2, lane, LANES), axis=-1, keepdims=True)
    e2 = jnp.exp(m2 - m1)
    den = 1.0 + e2
    ri_ref[...] = jnp.where(lane == 0, i1, jnp.where(lane == 1, i2, 0))
    rw_ref[...] = jnp.where(lane == 0, 1.0 / den, jnp.where(lane == 1, e2 / den, 0.0))


def _router(x, gpre, mod, wr_pad, rows):
    tm = RT_TM
    return pl.pallas_call(
        _router_kernel,
        out_shape=(jax.ShapeDtypeStruct((rows, D), F32),
                   jax.ShapeDtypeStruct((rows, LANES), jnp.int32),
                   jax.ShapeDtypeStruct((rows, LANES), F32)),
        grid=(rows // tm,),
        in_specs=[
            pl.BlockSpec((tm, D), lambda i: (i, 0)),
            pl.BlockSpec((1, D), lambda i: (0, 0)),
            _mod_spec(tm, 3),
            _mod_spec(tm, 4),
            pl.BlockSpec((D, LANES), lambda i: (0, 0)),
        ],
        out_specs=(pl.BlockSpec((tm, D), lambda i: (i, 0)),
                   pl.BlockSpec((tm, LANES), lambda i: (i, 0)),
                   pl.BlockSpec((tm, LANES), lambda i: (i, 0))),
        compiler_params=_cparams(("arbitrary",)),
        name="moe_router",
    )(x, gpre.reshape(1, D), mod, mod, wr_pad)


def _moe_kernel(te_ref, nsub_ref, src_ref, dst_ref, h_hbm, wg_ref, wu_ref, wd_ref, y_hbm,
                x_scr, stage, acc_scr, wgb, wub, wdb, sem_in, sem_out):
    s = pl.program_id(0)
    j = pl.program_id(1)
    sub = MOE_SUB
    nsub = nsub_ref[s]
    active = nsub > 0

    def rows_of(c):
        return pl.ds(pl.multiple_of(c * sub, sub), sub)

    @pl.when(jnp.logical_and(s == 0, j == 0))
    def _():
        stage[0] = jnp.zeros((sub, D), F32)
        dump = pltpu.make_async_copy(stage.at[0], y_hbm.at[pl.ds(y_hbm.shape[0] - sub, sub)], sem_out)
        dump.start()
        dump.wait()

    @pl.when(jnp.logical_and(active, j == 0))
    def _():
        def issue(c, slot):
            def one(r, carry):
                pltpu.make_async_copy(h_hbm.at[pl.ds(src_ref[0, 0, c * sub + r], 1)],
                                      stage.at[slot, pl.ds(r, 1)], sem_in.at[slot]).start()
                return carry
            lax.fori_loop(0, sub, one, 0, unroll=8)

        issue(0, 0)

        @pl.when(nsub > 1)
        def _():
            issue(1, 1)

        def chunk(c, slot):
            @pl.when(c + 2 < nsub)
            def _():
                issue(c + 2, jnp.where(slot == 0, 2, slot - 1))

            pltpu.make_async_copy(h_hbm.at[pl.ds(0, sub)], stage.at[slot], sem_in.at[slot]).wait()
            x_scr[rows_of(c)] = stage[slot].astype(BF16)
            acc_scr[rows_of(c)] = jnp.zeros((sub, D), F32)
            return jnp.where(slot == 2, 0, slot + 1)
        lax.fori_loop(0, nsub, chunk, 0)

    @pl.when(active)
    def _():
        wgb[...] = wg_ref[0, 0].astype(BF16)
        wub[...] = wu_ref[0, 0].astype(BF16)
        wdb[...] = wd_ref[0, 0].astype(BF16)

        def pair(p, carry):
            rows = pl.ds(pl.multiple_of(p * (2 * sub), 2 * sub), 2 * sub)
            acc_scr[rows] += _swiglu_step(x_scr[rows], wgb[...], wub[...], wdb[...])
            return carry
        lax.fori_loop(0, nsub >> 1, pair, 0)

        @pl.when((nsub & 1) == 1)
        def _():
            rows = rows_of(nsub - 1)
            acc_scr[rows] += _swiglu_step(x_scr[rows], wgb[...], wub[...], wdb[...])

    @pl.when(jnp.logical_and(active, j == pl.num_programs(1) - 1))
    def _():
        def issue(c, carry):
            def one(r, carry2):
                row = c * sub + r
                pltpu.make_async_copy(acc_scr.at[pl.ds(row, 1)], y_hbm.at[pl.ds(dst_ref[0, 0, row], 1)],
                                      sem_out).start()
                return carry2
            lax.fori_loop(0, sub, one, 0, unroll=8)
            return carry
        lax.fori_loop(0, nsub, issue, 0)

        def drain(c, carry):
            pltpu.make_async_copy(acc_scr.at[pl.ds(0, sub)], y_hbm.at[pl.ds(0, sub)], sem_out).wait()
            return carry
        lax.fori_loop(0, nsub, drain, 0)


def _moe_plan(ri, rows):
    sub, rs = MOE_SUB, MOE_RS
    n_asg = 2 * rows
    ns_max = (n_asg + NE * (sub - 1)) // rs + NE
    e_flat = jnp.concatenate([ri[:, 0], ri[:, 1]])
    _, order = lax.sort((e_flat, jnp.arange(n_asg, dtype=jnp.int32)), num_keys=1)
    counts = jnp.sum((e_flat[:, None] == jnp.arange(NE, dtype=jnp.int32)[None, :]).astype(jnp.int32), axis=0)
    ustart = jnp.cumsum(counts) - counts
    psz = ((counts + sub - 1) // sub) * sub
    nst = (psz + rs - 1) // rs
    st_end = jnp.cumsum(nst)
    total = st_end[NE - 1]
    s_idx = jnp.arange(ns_max, dtype=jnp.int32)
    valid = s_idx < total
    s_eff = jnp.minimum(s_idx, total - 1)
    e_s = jnp.sum((s_eff[:, None] >= st_end[None, :]).astype(jnp.int32), axis=1)
    k_s = s_eff - (st_end - nst)[e_s]
    nsub = jnp.where(valid, jnp.clip(psz[e_s] - k_s * rs, 0, rs) // sub, 0)
    lane = jnp.arange(rs, dtype=jnp.int32)[None, :]
    n_real = jnp.where(valid, jnp.clip(counts[e_s] - k_s * rs, 0, rs), 0)
    row_valid = lane < n_real[:, None]
    base = ustart[e_s] + k_s * rs
    order_pad = jnp.concatenate([order, jnp.zeros((rs,), jnp.int32)])
    asg = jnp.stack([lax.dynamic_slice(order_pad, (base[s],), (rs,)) for s in range(ns_max)])
    src = jnp.where(row_valid, asg % rows, 0)
    dst = jnp.where(row_valid, asg, n_asg + lane % sub)
    return (e_s.astype(jnp.int32), nsub.astype(jnp.int32),
            src.astype(jnp.int32).reshape(ns_max, 1, rs), dst.astype(jnp.int32).reshape(ns_max, 1, rs))


def _moe_experts(h, plan, wg, wu, wd, layer, rows):
    sub, rs, tf = MOE_SUB, MOE_RS, MOE_TF
    tile_e, nsub, src, dst = plan
    ns_max = src.shape[0]
    nj = FF // tf
    jmap = lambda j, ns, s: jnp.where(ns[s] > 0, j, nj - 1)
    grid_spec = pltpu.PrefetchScalarGridSpec(
        num_scalar_prefetch=2,
        grid=(ns_max, nj),
        in_specs=[
            pl.BlockSpec((1, 1, rs), lambda s, j, te, ns: (s, 0, 0), memory_space=pltpu.SMEM),
            pl.BlockSpec((1, 1, rs), lambda s, j, te, ns: (s, 0, 0), memory_space=pltpu.SMEM),
            pl.BlockSpec(memory_space=pl.ANY),
            pl.BlockSpec((1, 1, D, tf), lambda s, j, te, ns: (layer, te[s], 0, jmap(j, ns, s))),
            pl.BlockSpec((1, 1, D, tf), lambda s, j, te, ns: (layer, te[s], 0, jmap(j, ns, s))),
            pl.BlockSpec((1, 1, tf, D), lambda s, j, te, ns: (layer, te[s], jmap(j, ns, s), 0)),
        ],
        out_specs=pl.BlockSpec(memory_space=pl.ANY),
        scratch_shapes=[
            pltpu.VMEM((rs, D), BF16),
            pltpu.VMEM((3, sub, D), F32),
            pltpu.VMEM((rs, D), F32),
            pltpu.VMEM((D, tf), BF16),
            pltpu.VMEM((D, tf), BF16),
            pltpu.VMEM((tf, D), BF16),
            pltpu.SemaphoreType.DMA((3,)),
            pltpu.SemaphoreType.DMA(()),
        ],
    )
    return pl.pallas_call(
        _moe_kernel,
        out_shape=jax.ShapeDtypeStruct((2 * rows + sub, D), F32),
        grid_spec=grid_spec,
        compiler_params=_cparams(("arbitrary", "arbitrary")),
        name="moe_experts",
    )(tile_e, nsub, src, dst, h, wg, wu, wd)


POST_TM = 512


def _moe_post_kernel(x_ref, y0_ref, y1_ref, rw_ref, gpost_ref, gate_ref, o_ref):
    w = rw_ref[...]
    f = w[:, 0:1] * y0_ref[...] + w[:, 1:2] * y1_ref[...]
    o_ref[...] = x_ref[...] + gate_ref[0] * (_rms(f) * gpost_ref[...])


def _moe_post(x, y2, rw, gpost, mod, rows):
    tm = POST_TM
    off = rows // tm
    return pl.pallas_call(
        _moe_post_kernel,
        out_shape=jax.ShapeDtypeStruct((rows, D), F32),
        grid=(rows // tm,),
        in_specs=[
            pl.BlockSpec((tm, D), lambda i: (i, 0)),
            pl.BlockSpec((tm, D), lambda i: (i, 0)),
            pl.BlockSpec((tm, D), lambda i: (off + i, 0)),
            pl.BlockSpec((tm, LANES), lambda i: (i, 0)),
            pl.BlockSpec((1, D), lambda i: (0, 0)),
            _mod_spec(tm, 5),
        ],
        out_specs=pl.BlockSpec((tm, D), lambda i: (i, 0)),
        compiler_params=_cparams(("arbitrary",)),
        name="moe_post",
    )(x, y2, y2, rw, gpost.reshape(1, D), mod)


def _rope_tables():
    rows = SEQ // GRID_W
    row = jnp.repeat(jnp.arange(rows), GRID_W).astype(F32)
    col = jnp.tile(jnp.arange(GRID_W), rows).astype(F32)
    inv = ROPE_BASE ** (-jnp.arange(ROPE_PAIRS, dtype=F32) / ROPE_PAIRS)
    ang_r = row[:, None] * inv[None, :]
    ang_c = col[:, None] * inv[None, :]
    cr, sr, cc, sc = jnp.cos(ang_r), jnp.sin(ang_r), jnp.cos(ang_c), jnp.sin(ang_c)
    cos_t = jnp.concatenate([cr, cr, cc, cc], axis=-1)
    sin_t = jnp.concatenate([-sr, sr, -sc, sc], axis=-1)
    cos_t = jnp.concatenate([cos_t, jnp.ones((IN_TM, LANES), F32)], axis=0)
    sin_t = jnp.concatenate([sin_t, jnp.zeros((IN_TM, LANES), F32)], axis=0)
    return cos_t, sin_t


def kernel(x, c, ctx, c_ctx, w_mod, b_mod, g_attn_pre, g_attn_post, g_ffn_pre, g_ffn_post, w_in, w_out,
           sink_logit, lambda_q1, lambda_k1, lambda_q2, lambda_k2, subln_g, ffn_w_gate, ffn_w_up,
           ffn_w_down, moe_router, moe_w_gate, moe_w_up, moe_w_down):
    xs = jnp.concatenate([x.reshape(NL, D), ctx.reshape(NC, D)], axis=0)
    cin = jnp.concatenate([c, c_ctx[None, :], jnp.zeros((8 - B - 1, D), F32)], axis=0)
    mod_all = _modulation(cin, w_mod, b_mod)
    cos_t, sin_t = _rope_tables()

    for l in range(DEPTH):
        last = l == DEPTH - 1
        rows = NL if last else NR
        mod = mod_all[l].reshape(8, 1, N_MOD * D)
        lam_init = 0.8 - 0.6 * math.exp(-0.3 * l)
        lam_vecs = jnp.stack([lambda_q1[l], lambda_k1[l], lambda_q2[l], lambda_k2[l]])

        qkv = _in_proj(xs, g_attn_pre[l], mod, w_in[l].astype(BF16), cos_t, sin_t)
        attn_lat = _mixers(qkv, sink_logit[l], lam_vecs, subln_g[l], lam_init, lat=True)
        attn_ctx = None if last else _mixers(qkv, sink_logit[l], lam_vecs, subln_g[l], lam_init, lat=False)
        xs = _out_proj(xs, attn_lat, attn_ctx, w_out, l, g_attn_post[l], mod, rows)

        i = l // 2
        if l % 2 == 0:
            xs = _ffn_dense(xs, g_ffn_pre[l], g_ffn_post[l], mod, ffn_w_gate, ffn_w_up, ffn_w_down, i, rows)
        else:
            wr_pad = jnp.pad(moe_router[i], ((0, 0), (0, LANES - NE)))
            h, ri, rw = _router(xs, g_ffn_pre[l], mod, wr_pad, rows)
            plan = _moe_plan(ri, rows)
            y2 = _moe_experts(h, plan, moe_w_gate, moe_w_up, moe_w_down, i, rows)
            xs = _moe_post(xs, y2, rw, g_ffn_post[l], mod, rows)
    return xs.reshape(B, SEQ, D)
```

```python
import functools
import math

import jax
import jax.numpy as jnp
from jax import lax
from jax.experimental import pallas as pl
from jax.experimental.pallas import tpu as pltpu

D = 2048
B = 4
SEQ = 2048
DEPTH = 4
GRID_W = 64
CTX = 256
HD = 128
WINDOW = 128
A_HEADS = 8
A_KVH = 2
A_GROUP = 4
B_HEADS = 4
B_VDIM = 256
FF = 5632
NE = 8
N_MOD = 6
ROPE_BASE = 10000.0
ROPE_PAIRS = 32
ATTN_SCALE = HD ** -0.5
EPS = 1e-6
NEG = -1e30

NL = B * SEQ
NC = B * CTX
NR = NL + NC
IN_COLS = 4608
COL_QA, COL_QB, COL_KB, COL_VB, COL_KA, COL_VA = 0, 1024, 2048, 3072, 4096, 4352

LANES = 128
VMEM_LIMIT = 56 * 1024 * 1024

F32 = jnp.float32
BF16 = jnp.bfloat16


def _cparams(sem):
    return pltpu.CompilerParams(dimension_semantics=sem, vmem_limit_bytes=VMEM_LIMIT)


def _nt_dot(a, b):
    return lax.dot_general(a, b, (((1,), (1,)), ((), ())), preferred_element_type=F32)


def _rms(x):
    return x * lax.rsqrt(jnp.mean(x * x, axis=-1, keepdims=True) + EPS)


def _norm_mod(x, g, sh, sc):
    return (_rms(x) * g) * (1.0 + sc) + sh


def _mod_row(i, tm):
    return jnp.where(i < NL // tm, (i * tm) // SEQ, B)


def _mod_spec(tm, chunk, grid_rank=1):
    if grid_rank == 1:
        return pl.BlockSpec((1, 1, D), lambda i: (_mod_row(i, tm), 0, chunk))
    return pl.BlockSpec((1, 1, D), lambda i, j: (_mod_row(i, tm), 0, chunk))


MOD_TN = 1024


def _split_bf16(v):
    hi = v.astype(BF16)
    return hi, (v - hi.astype(F32)).astype(BF16)


def _mod_kernel(c_ref, w_ref, b_ref, o_ref):
    c = c_ref[...]
    a_hi, a_lo = _split_bf16(c * jax.nn.sigmoid(c))
    w_hi, w_lo = _split_bf16(w_ref[0])
    both = jnp.dot(jnp.concatenate([a_hi, a_lo], axis=0), w_hi, preferred_element_type=F32)
    o_ref[0] = both[0:8] + both[8:16] + jnp.dot(a_hi, w_lo, preferred_element_type=F32) + b_ref[0]


def _modulation(cin, w_mod, b_mod):
    return pl.pallas_call(
        _mod_kernel,
        out_shape=jax.ShapeDtypeStruct((DEPTH, 8, N_MOD * D), F32),
        grid=(DEPTH, N_MOD * D // MOD_TN),
        in_specs=[
            pl.BlockSpec((8, D), lambda l, n: (0, 0)),
            pl.BlockSpec((1, D, MOD_TN), lambda l, n: (l, 0, n)),
            pl.BlockSpec((1, 1, MOD_TN), lambda l, n: (l, 0, n)),
        ],
        out_specs=pl.BlockSpec((1, 8, MOD_TN), lambda l, n: (l, 0, n)),
        compiler_params=_cparams(("arbitrary", "arbitrary")),
        name="modulation",
    )(cin, w_mod, b_mod.reshape(DEPTH, 1, N_MOD * D))


IN_TM = 256
IN_CH = 512
LOG2E = math.log2(math.e)
Q_SCALE = ATTN_SCALE * LOG2E
IN_CHUNKS = ((0, 4, True), (1, 4, True), (8, 2, False), (2, 4, True), (3, 4, True),
             (4, 4, False), (5, 4, False), (6, 0, False), (7, 0, False))


def _inproj_kernel(x_ref, g_ref, sh_ref, sc_ref, cos_ref, sin_ref, w_ref, o_ref):
    hb = _norm_mod(x_ref[...], g_ref[...], sh_ref[0], sc_ref[0]).astype(BF16)
    cos = cos_ref[...]
    sin = sin_ref[...]
    lane = lax.broadcasted_iota(jnp.int32, (IN_TM, LANES), 1)
    first = (lane & 63) < 32
    for c, (dest, n_rope, is_q) in enumerate(IN_CHUNKS):
        acc = jnp.dot(hb, w_ref[:, c * IN_CH:(c + 1) * IN_CH], preferred_element_type=F32)
        for s in range(IN_CH // LANES):
            y = acc[:, s * LANES:(s + 1) * LANES]
            if s < n_rope:
                partner = jnp.where(first, pltpu.roll(y, 96, 1), pltpu.roll(y, 32, 1))
                y = y * cos + partner * sin
                if is_q:
                    y = y * Q_SCALE
            col = dest * IN_CH + s * LANES
            o_ref[:, col:col + LANES] = y.astype(BF16)


def _in_proj(x, g, mod, w_bf, cos_t, sin_t):
    rows = x.shape[0]
    tm = IN_TM
    tab = lambda i: (jnp.where(i < NL // tm, i % (SEQ // tm), SEQ // tm), 0)
    return pl.pallas_call(
        _inproj_kernel,
        out_shape=jax.ShapeDtypeStruct((rows, IN_COLS), BF16),
        grid=(rows // tm,),
        in_specs=[
            pl.BlockSpec((tm, D), lambda i: (i, 0)),
            pl.BlockSpec((1, D), lambda i: (0, 0)),
            _mod_spec(tm, 0),
            _mod_spec(tm, 1),
            pl.BlockSpec((tm, LANES), tab),
            pl.BlockSpec((tm, LANES), tab),
            pl.BlockSpec((D, IN_COLS), lambda i: (0, 0)),
        ],
        out_specs=pl.BlockSpec((tm, IN_COLS), lambda i: (i, 0)),
        compiler_params=_cparams(("arbitrary",)),
        name="in_proj",
    )(x, g.reshape(1, D), mod, mod, cos_t, sin_t, w_bf)


ATT_TQ_LAT = 256
ATT_TQ_CTX = 256


def _attn_a_body(sink_ref, q_ref, kc_ref, vc_ref, kl_ref, vl_ref, o_ref, local, kv_heads):
    tq = q_ref.shape[0]
    a_win = tq + 2 * WINDOW
    i = pl.program_id(1)
    for kv in kv_heads:
        ksl = slice(kv * HD, (kv + 1) * HD)
        qs = jnp.concatenate(
            [q_ref[:, (kv * A_GROUP + g) * HD:(kv * A_GROUP + g + 1) * HD] for g in range(A_GROUP)], axis=0)
        sink = jnp.concatenate(
            [jnp.full((tq, 1), sink_ref[kv * A_GROUP + g] * LOG2E, F32) for g in range(A_GROUP)], axis=0)
        s_ctx = _nt_dot(qs, kc_ref[:, ksl])
        m = jnp.maximum(jnp.max(s_ctx, axis=-1, keepdims=True), sink)
        if local:
            st = pl.multiple_of(jnp.clip(i * tq - WINDOW, 0, SEQ - a_win), WINDOW)
            s_loc = _nt_dot(qs, kl_ref[pl.ds(st, a_win), ksl])
            qpos = i * tq + (lax.broadcasted_iota(jnp.int32, (A_GROUP * tq, a_win), 0) & (tq - 1))
            kpos = st + lax.broadcasted_iota(jnp.int32, (A_GROUP * tq, a_win), 1)
            s_loc = jnp.where(jnp.abs(qpos - kpos) <= WINDOW, s_loc, NEG)
            m = jnp.maximum(m, jnp.max(s_loc, axis=-1, keepdims=True))
        p_ctx = jnp.exp2(s_ctx - m)
        den = jnp.sum(p_ctx, axis=-1, keepdims=True) + jnp.exp2(sink - m)
        o = jnp.dot(p_ctx.astype(BF16), vc_ref[:, ksl], preferred_element_type=F32)
        if local:
            p_loc = jnp.exp2(s_loc - m)
            den = den + jnp.sum(p_loc, axis=-1, keepdims=True)
            o = o + jnp.dot(p_loc.astype(BF16), vl_ref[pl.ds(st, a_win), ksl], preferred_element_type=F32)
        o = o * (1.0 / den)
        for g in range(A_GROUP):
            h = kv * A_GROUP + g
            o_ref[:, h * HD:(h + 1) * HD] = o[g * tq:(g + 1) * tq].astype(BF16)


def _attn_b_body(lam_ref, g_ref, q_ref, kc_ref, vc_ref, kl_ref, vl_ref, o_ref, with_lat, lam_init, heads):
    lv = lam_ref[...]
    lam = (jnp.exp(jnp.sum(lv[0:1] * lv[1:2], axis=-1, keepdims=True))
           - jnp.exp(jnp.sum(lv[2:3] * lv[3:4], axis=-1, keepdims=True)) + lam_init)
    for h in heads:
        ec, el, rden = [], [], []
        for mth in range(2):
            sl = slice((2 * h + mth) * HD, (2 * h + mth + 1) * HD)
            q = q_ref[:, sl]
            s_c = _nt_dot(q, kc_ref[:, sl])
            mx = jnp.max(s_c, axis=-1, keepdims=True)
            if with_lat:
                s_l = _nt_dot(q, kl_ref[:, sl])
                mx = jnp.maximum(mx, jnp.max(s_l, axis=-1, keepdims=True))
            e_c = jnp.exp2(s_c - mx)
            den = jnp.sum(e_c, axis=-1, keepdims=True)
            ec.append(e_c)
            if with_lat:
                e_l = jnp.exp2(s_l - mx)
                den = den + jnp.sum(e_l, axis=-1, keepdims=True)
                el.append(e_l)
            rden.append(1.0 / den)
        r1 = rden[0]
        r2 = lam * rden[1]
        vsl = slice(h * B_VDIM, (h + 1) * B_VDIM)
        o = jnp.dot((ec[0] * r1 - ec[1] * r2).astype(BF16), vc_ref[:, vsl], preferred_element_type=F32)
        if with_lat:
            o = o + jnp.dot((el[0] * r1 - el[1] * r2).astype(BF16), vl_ref[:, vsl],
                            preferred_element_type=F32)
        o = (_rms(o) * g_ref[...]) * (1.0 - lam_init)
        o_ref[:, vsl] = o.astype(BF16)


def _mixers_kernel(sink_ref, lam_ref, g_ref, qa_ref, kac_ref, vac_ref, qb_ref, kbc_ref, vbc_ref, *rest,
                   lat, lam_init):
    if lat:
        kal_ref, val_ref, kbl_ref, vbl_ref, oa_ref, ob_ref = rest
    else:
        oa_ref, ob_ref = rest
        kal_ref = val_ref = kbl_ref = vbl_ref = None
    mixer_a = functools.partial(_attn_a_body, sink_ref, qa_ref, kac_ref, vac_ref, kal_ref, val_ref, oa_ref, lat)
    mixer_b = functools.partial(_attn_b_body, lam_ref, g_ref, qb_ref, kbc_ref, vbc_ref, kbl_ref, vbl_ref, ob_ref,
                                lat, lam_init)
    mixer_a(range(A_KVH))
    mixer_b(range(B_HEADS))


def _mixers(qkv, sink, lam_vecs, subln_g, lam_init, lat):
    tq = ATT_TQ_LAT if lat else ATT_TQ_CTX
    nq = SEQ // tq if lat else CTX // tq
    row0 = 0 if lat else NL // tq
    ctx0 = NL // CTX
    qrow = lambda b, i: row0 + b * nq + i
    in_specs = [
        pl.BlockSpec(memory_space=pltpu.SMEM),
        pl.BlockSpec((4, HD), lambda b, i: (0, 0)),
        pl.BlockSpec((1, B_VDIM), lambda b, i: (0, 0)),
        pl.BlockSpec((tq, 1024), lambda b, i: (qrow(b, i), COL_QA // 1024)),
        pl.BlockSpec((CTX, 256), lambda b, i: (ctx0 + b, COL_KA // 256)),
        pl.BlockSpec((CTX, 256), lambda b, i: (ctx0 + b, COL_VA // 256)),
        pl.BlockSpec((tq, 1024), lambda b, i: (qrow(b, i), COL_QB // 1024)),
        pl.BlockSpec((CTX, 1024), lambda b, i: (ctx0 + b, COL_KB // 1024)),
        pl.BlockSpec((CTX, 1024), lambda b, i: (ctx0 + b, COL_VB // 1024)),
    ]
    args = [sink.reshape(A_HEADS), lam_vecs, subln_g.reshape(1, B_VDIM)] + [qkv] * 6
    if lat:
        in_specs += [
            pl.BlockSpec((SEQ, 256), lambda b, i: (b, COL_KA // 256)),
            pl.BlockSpec((SEQ, 256), lambda b, i: (b, COL_VA // 256)),
            pl.BlockSpec((SEQ, 1024), lambda b, i: (b, COL_KB // 1024)),
            pl.BlockSpec((SEQ, 1024), lambda b, i: (b, COL_VB // 1024)),
        ]
        args += [qkv] * 4
    rows = NL if lat else NC
    out = jax.ShapeDtypeStruct((rows, 1024), BF16)
    out_spec = pl.BlockSpec((tq, 1024), lambda b, i: (b * nq + i, 0))
    return pl.pallas_call(
        functools.partial(_mixers_kernel, lat=lat, lam_init=lam_init),
        out_shape=(out, out),
        grid=(B, nq),
        in_specs=in_specs,
        out_specs=(out_spec, out_spec),
        compiler_params=_cparams(("arbitrary", "arbitrary")),
        name="mixers_lat" if lat else "mixers_ctx",
    )(*args)


OUT_TM = 512


def _outproj_kernel(x_ref, w_ref, g_ref, gate_ref, oal_ref, obl_ref, *rest, has_ctx):
    if has_ctx:
        oac_ref, obc_ref, o_ref, wb_scr = rest
    else:
        o_ref, wb_scr = rest
    i = pl.program_id(0)

    @pl.when(i == 0)
    def _():
        wb_scr[...] = w_ref[0].astype(BF16)

    def project(oa_ref, ob_ref):
        y = (jnp.dot(oa_ref[...], wb_scr[0:1024, :], preferred_element_type=F32)
             + jnp.dot(ob_ref[...], wb_scr[1024:2048, :], preferred_element_type=F32))
        o_ref[...] = x_ref[...] + gate_ref[0] * (_rms(y) * g_ref[...])

    if has_ctx:
        n_lat = NL // OUT_TM

        @pl.when(i < n_lat)
        def _():
            project(oal_ref, obl_ref)

        @pl.when(i >= n_lat)
        def _():
            project(oac_ref, obc_ref)
    else:
        project(oal_ref, obl_ref)


def _out_proj(x, attn_lat, attn_ctx, w_out, layer, g, mod, rows):
    tm = OUT_TM
    n_lat = NL // tm
    has_ctx = attn_ctx is not None
    lat_spec = pl.BlockSpec((tm, 1024), lambda i: (jnp.minimum(i, n_lat - 1), 0))
    ctx_spec = pl.BlockSpec((tm, 1024), lambda i: (jnp.maximum(i - n_lat, 0), 0))
    in_specs = [
        pl.BlockSpec((tm, D), lambda i: (i, 0)),
        pl.BlockSpec((1, D, D), lambda i: (layer, 0, 0), pipeline_mode=pl.Buffered(1)),
        pl.BlockSpec((1, D), lambda i: (0, 0)),
        _mod_spec(tm, 2),
        lat_spec, lat_spec,
    ]
    args = [x, w_out, g.reshape(1, D), mod, *attn_lat]
    if has_ctx:
        in_specs += [ctx_spec, ctx_spec]
        args += list(attn_ctx)
    return pl.pallas_call(
        functools.partial(_outproj_kernel, has_ctx=has_ctx),
        out_shape=jax.ShapeDtypeStruct((rows, D), F32),
        grid=(rows // tm,),
        in_specs=in_specs,
        out_specs=pl.BlockSpec((tm, D), lambda i: (i, 0)),
        scratch_shapes=[pltpu.VMEM((D, D), BF16)],
        compiler_params=_cparams(("arbitrary",)),
        name="out_proj",
    )(*args)


FFN_TM = 1024
FFN_TF = 256
FFN_HALF = 512


def _swiglu_step(hb, wg, wu, wd):
    gt = jnp.dot(hb, wg, preferred_element_type=F32)
    up = jnp.dot(hb, wu, preferred_element_type=F32)
    act = (gt * jax.nn.sigmoid(gt) * up).astype(BF16)
    return jnp.dot(act, wd, preferred_element_type=F32)


def _ffn_kernel(x_ref, gpre_ref, sh_ref, sc_ref, gate_ref, gpost_ref, wg_ref, wu_ref, wd_ref, o_ref,
                h_scr, wgb, wub, wdb):
    j = pl.program_id(1)

    @pl.when(j == 0)
    def _():
        h_scr[...] = _norm_mod(x_ref[...], gpre_ref[...], sh_ref[0], sc_ref[0]).astype(BF16)
        o_ref[...] = jnp.zeros_like(o_ref)

    wgb[...] = wg_ref[0].astype(BF16)
    wub[...] = wu_ref[0].astype(BF16)
    wdb[...] = wd_ref[0].astype(BF16)
    for r in range(FFN_TM // FFN_HALF):
        rows = slice(r * FFN_HALF, (r + 1) * FFN_HALF)
        o_ref[rows, :] += _swiglu_step(h_scr[rows, :], wgb[...], wub[...], wdb[...])

    @pl.when(j == pl.num_programs(1) - 1)
    def _():
        o_ref[...] = x_ref[...] + gate_ref[0] * (_rms(o_ref[...]) * gpost_ref[...])


def _ffn_dense(x, gpre, gpost, mod, wg, wu, wd, layer, rows):
    tm, tf = FFN_TM, FFN_TF
    return pl.pallas_call(
        _ffn_kernel,
        out_shape=jax.ShapeDtypeStruct((rows, D), F32),
        grid=(rows // tm, FF // tf),
        in_specs=[
            pl.BlockSpec((tm, D), lambda i, j: (i, 0), pipeline_mode=pl.Buffered(1)),
            pl.BlockSpec((1, D), lambda i, j: (0, 0)),
            _mod_spec(tm, 3, 2),
            _mod_spec(tm, 4, 2),
            _mod_spec(tm, 5, 2),
            pl.BlockSpec((1, D), lambda i, j: (0, 0)),
            pl.BlockSpec((1, D, tf), lambda i, j: (layer, 0, j)),
            pl.BlockSpec((1, D, tf), lambda i, j: (layer, 0, j)),
            pl.BlockSpec((1, tf, D), lambda i, j: (layer, j, 0)),
        ],
        out_specs=pl.BlockSpec((tm, D), lambda i, j: (i, 0)),
        scratch_shapes=[pltpu.VMEM((tm, D), BF16), pltpu.VMEM((D, tf), BF16), pltpu.VMEM((D, tf), BF16),
                        pltpu.VMEM((tf, D), BF16)],
        compiler_params=_cparams(("arbitrary", "arbitrary")),
        name="ffn_dense",
    )(x, gpre.reshape(1, D), mod, mod, mod, gpost.reshape(1, D), wg, wu, wd)


RT_TM = 512
MOE_SUB = 256
MOE_RS = 2560
MOE_TF = 256


def _router_kernel(x_ref, gpre_ref, sh_ref, sc_ref, wr_ref, h_ref, ri_ref, rw_ref):
    h = _norm_mod(x_ref[...], gpre_ref[...], sh_ref[0], sc_ref[0])
    h_ref[...] = h
    h_hi, h_lo = _split_bf16(h)
    w_hi, w_lo = _split_bf16(wr_ref[...])
    logits = (jnp.dot(h_hi, w_hi, preferred_element_type=F32) + jnp.dot(h_lo, w_hi, preferred_element_type=F32)
              + jnp.dot(h_hi, w_lo, preferred_element_type=F32))
    lane = lax.broadcasted_iota(jnp.int32, logits.shape, 1)
    logits = jnp.where(lane < NE, logits, -jnp.inf)
    m1 = jnp.max(logits, axis=-1, keepdims=True)
    i1 = jnp.min(jnp.where(logits == m1, lane, LANES), axis=-1, keepdims=True)
    rest = jnp.where(lane == i1, -jnp.inf, logits)
    m2 = jnp.max(rest, axis=-1, keepdims=True)
    i2 = jnp.min(jnp.where(rest == m2, lane, LANES), axis=-1, keepdims=True)
    e2 = jnp.exp(m2 - m1)
    den = 1.0 + e2
    ri_ref[...] = jnp.where(lane == 0, i1, jnp.where(lane == 1, i2, 0))
    rw_ref[...] = jnp.where(lane == 0, 1.0 / den, jnp.where(lane == 1, e2 / den, 0.0))


def _router(x, gpre, mod, wr_pad, rows):
    tm = RT_TM
    return pl.pallas_call(
        _router_kernel,
        out_shape=(jax.ShapeDtypeStruct((rows, D), F32),
                   jax.ShapeDtypeStruct((rows, LANES), jnp.int32),
                   jax.ShapeDtypeStruct((rows, LANES), F32)),
        grid=(rows // tm,),
        in_specs=[
            pl.BlockSpec((tm, D), lambda i: (i, 0)),
            pl.BlockSpec((1, D), lambda i: (0, 0)),
            _mod_spec(tm, 3),
            _mod_spec(tm, 4),
            pl.BlockSpec((D, LANES), lambda i: (0, 0)),
        ],
        out_specs=(pl.BlockSpec((tm, D), lambda i: (i, 0)),
                   pl.BlockSpec((tm, LANES), lambda i: (i, 0)),
                   pl.BlockSpec((tm, LANES), lambda i: (i, 0))),
        compiler_params=_cparams(("arbitrary",)),
        name="moe_router",
    )(x, gpre.reshape(1, D), mod, mod, wr_pad)


def _moe_kernel(te_ref, nsub_ref, src_ref, dst_ref, h_hbm, wg_ref, wu_ref, wd_ref, y_hbm,
                x_scr, stage, acc_scr, wgb, wub, wdb, sem_in, sem_out):
    s = pl.program_id(0)
    j = pl.program_id(1)
    sub = MOE_SUB
    nsub = nsub_ref[s]
    active = nsub > 0

    def rows_of(c):
        return pl.ds(pl.multiple_of(c * sub, sub), sub)

    @pl.when(jnp.logical_and(s == 0, j == 0))
    def _():
        stage[0] = jnp.zeros((sub, D), F32)
        dump = pltpu.make_async_copy(stage.at[0], y_hbm.at[pl.ds(y_hbm.shape[0] - sub, sub)], sem_out)
        dump.start()
        dump.wait()

    @pl.when(jnp.logical_and(active, j == 0))
    def _():
        def issue(c, slot):
            def one(r, carry):
                pltpu.make_async_copy(h_hbm.at[pl.ds(src_ref[0, 0, c * sub + r], 1)],
                                      stage.at[slot, pl.ds(r, 1)], sem_in.at[slot]).start()
                return carry
            lax.fori_loop(0, sub, one, 0, unroll=8)

        issue(0, 0)

        @pl.when(nsub > 1)
        def _():
            issue(1, 1)

        def chunk(c, slot):
            @pl.when(c + 2 < nsub)
            def _():
                issue(c + 2, jnp.where(slot == 0, 2, slot - 1))

            pltpu.make_async_copy(h_hbm.at[pl.ds(0, sub)], stage.at[slot], sem_in.at[slot]).wait()
            x_scr[rows_of(c)] = stage[slot].astype(BF16)
            acc_scr[rows_of(c)] = jnp.zeros((sub, D), F32)
            return jnp.where(slot == 2, 0, slot + 1)
        lax.fori_loop(0, nsub, chunk, 0)

    @pl.when(active)
    def _():
        wgb[...] = wg_ref[0, 0].astype(BF16)
        wub[...] = wu_ref[0, 0].astype(BF16)
        wdb[...] = wd_ref[0, 0].astype(BF16)

        def block(start, n_chunks):
            rows = pl.ds(pl.multiple_of(start * sub, sub), n_chunks * sub)
            acc_scr[rows] += _swiglu_step(x_scr[rows], wgb[...], wub[...], wdb[...])

        def quad(q, carry):
            block(q * 4, 2)
            block(q * 4 + 2, 2)
            return carry
        lax.fori_loop(0, nsub >> 2, quad, 0)

        @pl.when((nsub & 2) == 2)
        def _():
            block((nsub >> 2) * 4, 2)

        @pl.when((nsub & 1) == 1)
        def _():
            block(nsub - 1, 1)

    @pl.when(jnp.logical_and(active, j == pl.num_programs(1) - 1))
    def _():
        def issue(g, carry):
            base = pl.multiple_of(g * 8, 8)
            for u in range(8):
                pltpu.make_async_copy(acc_scr.at[pl.ds(base + u, 1)],
                                      y_hbm.at[pl.ds(dst_ref[0, 0, base + u], 1)], sem_out).start()
            return carry
        lax.fori_loop(0, nsub * (sub // 8), issue, 0)

        def drain(c, carry):
            pltpu.make_async_copy(acc_scr.at[pl.ds(0, sub)], y_hbm.at[pl.ds(0, sub)], sem_out).wait()
            return carry
        lax.fori_loop(0, nsub, drain, 0)


def _moe_plan(ri, rows):
    sub, rs = MOE_SUB, MOE_RS
    n_asg = 2 * rows
    ns_max = (n_asg + NE * (sub - 1)) // rs + NE
    e_flat = jnp.concatenate([ri[:, 0], ri[:, 1]])
    _, order = lax.sort((e_flat, jnp.arange(n_asg, dtype=jnp.int32)), num_keys=1)
    counts = jnp.sum((e_flat[:, None] == jnp.arange(NE, dtype=jnp.int32)[None, :]).astype(jnp.int32), axis=0)
    ustart = jnp.cumsum(counts) - counts
    psz = ((counts + sub - 1) // sub) * sub
    nst = (psz + rs - 1) // rs
    st_end = jnp.cumsum(nst)
    total = st_end[NE - 1]
    s_idx = jnp.arange(ns_max, dtype=jnp.int32)
    valid = s_idx < total
    s_eff = jnp.minimum(s_idx, total - 1)
    e_s = jnp.sum((s_eff[:, None] >= st_end[None, :]).astype(jnp.int32), axis=1)
    k_s = s_eff - (st_end - nst)[e_s]
    nsub = jnp.where(valid, jnp.clip(psz[e_s] - k_s * rs, 0, rs) // sub, 0)
    lane = jnp.arange(rs, dtype=jnp.int32)[None, :]
    n_real = jnp.where(valid, jnp.clip(counts[e_s] - k_s * rs, 0, rs), 0)
    row_valid = lane < n_real[:, None]
    base = ustart[e_s] + k_s * rs
    order_pad = jnp.concatenate([order, jnp.zeros((rs,), jnp.int32)])
    asg = jnp.stack([lax.dynamic_slice(order_pad, (base[s],), (rs,)) for s in range(ns_max)])
    src = jnp.where(row_valid, asg % rows, 0)
    dst = jnp.where(row_valid, asg, n_asg + lane % sub)
    plan = (e_s.astype(jnp.int32), nsub.astype(jnp.int32),
            src.astype(jnp.int32).reshape(ns_max, 1, rs), dst.astype(jnp.int32).reshape(ns_max, 1, rs))
    return plan, total


def _moe_experts(h, plan, wg, wu, wd, layer, rows):
    sub, rs, tf = MOE_SUB, MOE_RS, MOE_TF
    tile_e, nsub, src, dst = plan
    ns_max = src.shape[0]
    nj = FF // tf
    jmap = lambda j, ns, s: jnp.where(ns[s] > 0, j, nj - 1)
    grid_spec = pltpu.PrefetchScalarGridSpec(
        num_scalar_prefetch=2,
        grid=(ns_max, nj),
        in_specs=[
            pl.BlockSpec((1, 1, rs), lambda s, j, te, ns: (s, 0, 0), memory_space=pltpu.SMEM),
            pl.BlockSpec((1, 1, rs), lambda s, j, te, ns: (s, 0, 0), memory_space=pltpu.SMEM),
            pl.BlockSpec(memory_space=pl.ANY),
            pl.BlockSpec((1, 1, D, tf), lambda s, j, te, ns: (layer, te[s], 0, jmap(j, ns, s))),
            pl.BlockSpec((1, 1, D, tf), lambda s, j, te, ns: (layer, te[s], 0, jmap(j, ns, s))),
            pl.BlockSpec((1, 1, tf, D), lambda s, j, te, ns: (layer, te[s], jmap(j, ns, s), 0)),
        ],
        out_specs=pl.BlockSpec(memory_space=pl.ANY),
        scratch_shapes=[
            pltpu.VMEM((rs, D), BF16),
            pltpu.VMEM((3, sub, D), F32),
            pltpu.VMEM((rs, D), F32),
            pltpu.VMEM((D, tf), BF16),
            pltpu.VMEM((D, tf), BF16),
            pltpu.VMEM((tf, D), BF16),
            pltpu.SemaphoreType.DMA((3,)),
            pltpu.SemaphoreType.DMA(()),
        ],
    )
    return pl.pallas_call(
        _moe_kernel,
        out_shape=jax.ShapeDtypeStruct((2 * rows + sub, D), F32),
        grid_spec=grid_spec,
        compiler_params=_cparams(("arbitrary", "arbitrary")),
        name="moe_experts",
    )(tile_e, nsub, src, dst, h, wg, wu, wd)


POST_TM = 512


def _moe_post_kernel(x_ref, y0_ref, y1_ref, rw_ref, gpost_ref, gate_ref, o_ref):
    w = rw_ref[...]
    f = w[:, 0:1] * y0_ref[...] + w[:, 1:2] * y1_ref[...]
    o_ref[...] = x_ref[...] + gate_ref[0] * (_rms(f) * gpost_ref[...])


def _moe_post(x, y2, rw, gpost, mod, rows):
    tm = POST_TM
    off = rows // tm
    return pl.pallas_call(
        _moe_post_kernel,
        out_shape=jax.ShapeDtypeStruct((rows, D), F32),
        grid=(rows // tm,),
        in_specs=[
            pl.BlockSpec((tm, D), lambda i: (i, 0)),
            pl.BlockSpec((tm, D), lambda i: (i, 0)),
            pl.BlockSpec((tm, D), lambda i: (off + i, 0)),
            pl.BlockSpec((tm, LANES), lambda i: (i, 0)),
            pl.BlockSpec((1, D), lambda i: (0, 0)),
            _mod_spec(tm, 5),
        ],
        out_specs=pl.BlockSpec((tm, D), lambda i: (i, 0)),
        compiler_params=_cparams(("arbitrary",)),
        name="moe_post",
    )(x, y2, y2, rw, gpost.reshape(1, D), mod)


def _rope_tables():
    rows = SEQ // GRID_W
    row = jnp.repeat(jnp.arange(rows), GRID_W).astype(F32)
    col = jnp.tile(jnp.arange(GRID_W), rows).astype(F32)
    inv = ROPE_BASE ** (-jnp.arange(ROPE_PAIRS, dtype=F32) / ROPE_PAIRS)
    ang_r = row[:, None] * inv[None, :]
    ang_c = col[:, None] * inv[None, :]
    cr, sr, cc, sc = jnp.cos(ang_r), jnp.sin(ang_r), jnp.cos(ang_c), jnp.sin(ang_c)
    cos_t = jnp.concatenate([cr, cr, cc, cc], axis=-1)
    sin_t = jnp.concatenate([-sr, sr, -sc, sc], axis=-1)
    cos_t = jnp.concatenate([cos_t, jnp.ones((IN_TM, LANES), F32)], axis=0)
    sin_t = jnp.concatenate([sin_t, jnp.zeros((IN_TM, LANES), F32)], axis=0)
    return cos_t, sin_t


def kernel(x, c, ctx, c_ctx, w_mod, b_mod, g_attn_pre, g_attn_post, g_ffn_pre, g_ffn_post, w_in, w_out,
           sink_logit, lambda_q1, lambda_k1, lambda_q2, lambda_k2, subln_g, ffn_w_gate, ffn_w_up,
           ffn_w_down, moe_router, moe_w_gate, moe_w_up, moe_w_down):
    xs = jnp.concatenate([x.reshape(NL, D), ctx.reshape(NC, D)], axis=0)
    cin = jnp.concatenate([c, c_ctx[None, :], jnp.zeros((8 - B - 1, D), F32)], axis=0)
    mod_all = _modulation(cin, w_mod, b_mod)
    cos_t, sin_t = _rope_tables()

    for l in range(DEPTH):
        last = l == DEPTH - 1
        rows = NL if last else NR
        mod = mod_all[l].reshape(8, 1, N_MOD * D)
        lam_init = 0.8 - 0.6 * math.exp(-0.3 * l)
        lam_vecs = jnp.stack([lambda_q1[l], lambda_k1[l], lambda_q2[l], lambda_k2[l]])

        qkv = _in_proj(xs, g_attn_pre[l], mod, w_in[l].astype(BF16), cos_t, sin_t)
        attn_lat = _mixers(qkv, sink_logit[l], lam_vecs, subln_g[l], lam_init, lat=True)
        attn_ctx = None if last else _mixers(qkv, sink_logit[l], lam_vecs, subln_g[l], lam_init, lat=False)
        xs = _out_proj(xs, attn_lat, attn_ctx, w_out, l, g_attn_post[l], mod, rows)

        i = l // 2
        if l % 2 == 0:
            xs = _ffn_dense(xs, g_ffn_pre[l], g_ffn_post[l], mod, ffn_w_gate, ffn_w_up, ffn_w_down, i, rows)
        else:
            wr_pad = jnp.pad(moe_router[i], ((0, 0), (0, LANES - NE)))
            h, ri, rw = _router(xs, g_ffn_pre[l], mod, wr_pad, rows)
            plan, n_super = _moe_plan(ri, rows)

            def experts(n, plan=plan, h=h, i=i, rows=rows):
                return _moe_experts(h, tuple(p[:n] for p in plan), moe_w_gate, moe_w_up, moe_w_down, i, rows)

            y2 = lax.cond(n_super <= NE, functools.partial(experts, NE),
                          functools.partial(experts, plan[0].shape[0]))
            xs = _moe_post(xs, y2, rw, g_ffn_post[l], mod, rows)
    return xs.reshape(B, SEQ, D)
```

```python
import functools
import math

import jax
import jax.numpy as jnp
from jax import lax
from jax.experimental import pallas as pl
from jax.experimental.pallas import tpu as pltpu

D = 2048
B = 4
SEQ = 2048
DEPTH = 4
GRID_W = 64
CTX = 256
HD = 128
WINDOW = 128
A_HEADS = 8
A_KVH = 2
A_GROUP = 4
B_HEADS = 4
B_VDIM = 256
FF = 5632
NE = 8
N_MOD = 6
ROPE_BASE = 10000.0
ROPE_PAIRS = 32
ATTN_SCALE = HD ** -0.5
EPS = 1e-6
NEG = -1e30

NL = B * SEQ
NC = B * CTX
NR = NL + NC
IN_COLS = 4608
COL_QA, COL_QB, COL_KB, COL_VB, COL_KA, COL_VA = 0, 1024, 2048, 3072, 4096, 4352

LANES = 128
VMEM_LIMIT = 56 * 1024 * 1024

F32 = jnp.float32
BF16 = jnp.bfloat16


def _cparams(sem):
    return pltpu.CompilerParams(dimension_semantics=sem, vmem_limit_bytes=VMEM_LIMIT)


def _nt_dot(a, b):
    return lax.dot_general(a, b, (((1,), (1,)), ((), ())), preferred_element_type=F32)


def _rms(x):
    return x * lax.rsqrt(jnp.mean(x * x, axis=-1, keepdims=True) + EPS)


def _norm_mod(x, g, sh, sc):
    return (_rms(x) * g) * (1.0 + sc) + sh


def _mod_row(i, tm):
    return jnp.where(i < NL // tm, (i * tm) // SEQ, B)


def _mod_spec(tm, chunk, grid_rank=1):
    if grid_rank == 1:
        return pl.BlockSpec((1, 1, D), lambda i: (_mod_row(i, tm), 0, chunk))
    return pl.BlockSpec((1, 1, D), lambda i, j: (_mod_row(i, tm), 0, chunk))


MOD_TN = 1024


def _split_bf16(v):
    hi = v.astype(BF16)
    return hi, (v - hi.astype(F32)).astype(BF16)


def _mod_kernel(c_ref, w_ref, b_ref, o_ref):
    c = c_ref[...]
    a_hi, a_lo = _split_bf16(c * jax.nn.sigmoid(c))
    w_hi, w_lo = _split_bf16(w_ref[0])
    both = jnp.dot(jnp.concatenate([a_hi, a_lo], axis=0), w_hi, preferred_element_type=F32)
    o_ref[0] = both[0:8] + both[8:16] + jnp.dot(a_hi, w_lo, preferred_element_type=F32) + b_ref[0]


def _modulation(cin, w_mod, b_mod):
    return pl.pallas_call(
        _mod_kernel,
        out_shape=jax.ShapeDtypeStruct((DEPTH, 8, N_MOD * D), F32),
        grid=(DEPTH, N_MOD * D // MOD_TN),
        in_specs=[
            pl.BlockSpec((8, D), lambda l, n: (0, 0)),
            pl.BlockSpec((1, D, MOD_TN), lambda l, n: (l, 0, n)),
            pl.BlockSpec((1, 1, MOD_TN), lambda l, n: (l, 0, n)),
        ],
        out_specs=pl.BlockSpec((1, 8, MOD_TN), lambda l, n: (l, 0, n)),
        compiler_params=_cparams(("arbitrary", "arbitrary")),
        name="modulation",
    )(cin, w_mod, b_mod.reshape(DEPTH, 1, N_MOD * D))


IN_TM = 256
IN_CH = 512
LOG2E = math.log2(math.e)
Q_SCALE = ATTN_SCALE * LOG2E
IN_CHUNKS = ((0, 4, True), (1, 4, True), (8, 2, False), (2, 4, True), (3, 4, True),
             (4, 4, False), (5, 4, False), (6, 0, False), (7, 0, False))


def _inproj_kernel(x_ref, g_ref, sh_ref, sc_ref, cos_ref, sin_ref, w_ref, o_ref):
    hb = _norm_mod(x_ref[...], g_ref[...], sh_ref[0], sc_ref[0]).astype(BF16)
    cos = cos_ref[...]
    sin = sin_ref[...]
    lane = lax.broadcasted_iota(jnp.int32, (IN_TM, LANES), 1)
    first = (lane & 63) < 32
    for c, (dest, n_rope, is_q) in enumerate(IN_CHUNKS):
        acc = jnp.dot(hb, w_ref[0, :, c * IN_CH:(c + 1) * IN_CH], preferred_element_type=F32)
        for s in range(IN_CH // LANES):
            y = acc[:, s * LANES:(s + 1) * LANES]
            if s < n_rope:
                partner = jnp.where(first, pltpu.roll(y, 96, 1), pltpu.roll(y, 32, 1))
                y = y * cos + partner * sin
                if is_q:
                    y = y * Q_SCALE
            col = dest * IN_CH + s * LANES
            o_ref[:, col:col + LANES] = y.astype(BF16)


def _in_proj(x, g, mod, w_bf, layer, cos_t, sin_t):
    rows = x.shape[0]
    tm = IN_TM
    tab = lambda i: (jnp.where(i < NL // tm, i % (SEQ // tm), SEQ // tm), 0)
    return pl.pallas_call(
        _inproj_kernel,
        out_shape=jax.ShapeDtypeStruct((rows, IN_COLS), BF16),
        grid=(rows // tm,),
        in_specs=[
            pl.BlockSpec((tm, D), lambda i: (i, 0)),
            pl.BlockSpec((1, D), lambda i: (0, 0)),
            _mod_spec(tm, 0),
            _mod_spec(tm, 1),
            pl.BlockSpec((tm, LANES), tab),
            pl.BlockSpec((tm, LANES), tab),
            pl.BlockSpec((1, D, IN_COLS), lambda i: (layer, 0, 0)),
        ],
        out_specs=pl.BlockSpec((tm, IN_COLS), lambda i: (i, 0)),
        compiler_params=_cparams(("arbitrary",)),
        name="in_proj",
    )(x, g.reshape(1, D), mod, mod, cos_t, sin_t, w_bf)


ATT_TQ_LAT = 256
ATT_TQ_CTX = 256


def _attn_a_body(sink_ref, q_ref, kc_ref, vc_ref, kl_ref, vl_ref, o_ref, local, kv_heads):
    tq = q_ref.shape[0]
    a_win = tq + 2 * WINDOW
    i = pl.program_id(1)
    for kv in kv_heads:
        ksl = slice(kv * HD, (kv + 1) * HD)
        qs = jnp.concatenate(
            [q_ref[:, (kv * A_GROUP + g) * HD:(kv * A_GROUP + g + 1) * HD] for g in range(A_GROUP)], axis=0)
        sink = jnp.concatenate(
            [jnp.full((tq, 1), sink_ref[kv * A_GROUP + g] * LOG2E, F32) for g in range(A_GROUP)], axis=0)
        s_ctx = _nt_dot(qs, kc_ref[:, ksl])
        m = jnp.maximum(jnp.max(s_ctx, axis=-1, keepdims=True), sink)
        if local:
            st = pl.multiple_of(jnp.clip(i * tq - WINDOW, 0, SEQ - a_win), WINDOW)
            s_loc = _nt_dot(qs, kl_ref[pl.ds(st, a_win), ksl])
            qpos = i * tq + (lax.broadcasted_iota(jnp.int32, (A_GROUP * tq, a_win), 0) & (tq - 1))
            kpos = st + lax.broadcasted_iota(jnp.int32, (A_GROUP * tq, a_win), 1)
            s_loc = jnp.where(jnp.abs(qpos - kpos) <= WINDOW, s_loc, NEG)
            m = jnp.maximum(m, jnp.max(s_loc, axis=-1, keepdims=True))
        p_ctx = jnp.exp2(s_ctx - m)
        den = jnp.sum(p_ctx, axis=-1, keepdims=True) + jnp.exp2(sink - m)
        o = jnp.dot(p_ctx.astype(BF16), vc_ref[:, ksl], preferred_element_type=F32)
        if local:
            p_loc = jnp.exp2(s_loc - m)
            den = den + jnp.sum(p_loc, axis=-1, keepdims=True)
            o = o + jnp.dot(p_loc.astype(BF16), vl_ref[pl.ds(st, a_win), ksl], preferred_element_type=F32)
        o = o * (1.0 / den)
        for g in range(A_GROUP):
            h = kv * A_GROUP + g
            o_ref[:, h * HD:(h + 1) * HD] = o[g * tq:(g + 1) * tq].astype(BF16)


def _attn_b_body(lam_ref, g_ref, q_ref, kc_ref, vc_ref, kl_ref, vl_ref, o_ref, with_lat, lam_init, heads):
    lv = lam_ref[...]
    lam = (jnp.exp(jnp.sum(lv[0:1] * lv[1:2], axis=-1, keepdims=True))
           - jnp.exp(jnp.sum(lv[2:3] * lv[3:4], axis=-1, keepdims=True)) + lam_init)
    for h in heads:
        vsl = slice(h * B_VDIM, (h + 1) * B_VDIM)
        pv, rden, scores = [], [], []
        for mth in range(2):
            sl = slice((2 * h + mth) * HD, (2 * h + mth + 1) * HD)
            q = q_ref[:, sl]
            scores.append((_nt_dot(q, kc_ref[:, sl]), _nt_dot(q, kl_ref[:, sl]) if with_lat else None))
        for mth in range(2):
            s_c, s_l = scores[mth]
            mx = jnp.max(s_c, axis=-1, keepdims=True)
            if with_lat:
                mx = jnp.maximum(mx, jnp.max(s_l, axis=-1, keepdims=True))
            e_c = jnp.exp2(s_c - mx)
            den = jnp.sum(e_c, axis=-1, keepdims=True)
            o_m = jnp.dot(e_c.astype(BF16), vc_ref[:, vsl], preferred_element_type=F32)
            if with_lat:
                e_l = jnp.exp2(s_l - mx)
                den = den + jnp.sum(e_l, axis=-1, keepdims=True)
                o_m = o_m + jnp.dot(e_l.astype(BF16), vl_ref[:, vsl], preferred_element_type=F32)
            pv.append(o_m)
            rden.append(1.0 / den)
        o = pv[0] * rden[0] - pv[1] * (lam * rden[1])
        o = (_rms(o) * g_ref[...]) * (1.0 - lam_init)
        o_ref[:, vsl] = o.astype(BF16)


def _mixers_kernel(sink_ref, lam_ref, g_ref, qa_ref, kac_ref, vac_ref, qb_ref, kbc_ref, vbc_ref, *rest,
                   lat, lam_init):
    if lat:
        kal_ref, val_ref, kbl_ref, vbl_ref, oa_ref, ob_ref = rest
    else:
        oa_ref, ob_ref = rest
        kal_ref = val_ref = kbl_ref = vbl_ref = None
    mixer_a = functools.partial(_attn_a_body, sink_ref, qa_ref, kac_ref, vac_ref, kal_ref, val_ref, oa_ref, lat)
    mixer_b = functools.partial(_attn_b_body, lam_ref, g_ref, qb_ref, kbc_ref, vbc_ref, kbl_ref, vbl_ref, ob_ref,
                                lat, lam_init)
    mixer_a(range(A_KVH))
    mixer_b(range(B_HEADS))


def _mixers(qkv, sink, lam_vecs, subln_g, lam_init, lat):
    tq = ATT_TQ_LAT if lat else ATT_TQ_CTX
    nq = SEQ // tq if lat else CTX // tq
    row0 = 0 if lat else NL // tq
    ctx0 = NL // CTX
    qrow = lambda b, i: row0 + b * nq + i
    in_specs = [
        pl.BlockSpec(memory_space=pltpu.SMEM),
        pl.BlockSpec((4, HD), lambda b, i: (0, 0)),
        pl.BlockSpec((1, B_VDIM), lambda b, i: (0, 0)),
        pl.BlockSpec((tq, 1024), lambda b, i: (qrow(b, i), COL_QA // 1024)),
        pl.BlockSpec((CTX, 256), lambda b, i: (ctx0 + b, COL_KA // 256)),
        pl.BlockSpec((CTX, 256), lambda b, i: (ctx0 + b, COL_VA // 256)),
        pl.BlockSpec((tq, 1024), lambda b, i: (qrow(b, i), COL_QB // 1024)),
        pl.BlockSpec((CTX, 1024), lambda b, i: (ctx0 + b, COL_KB // 1024)),
        pl.BlockSpec((CTX, 1024), lambda b, i: (ctx0 + b, COL_VB // 1024)),
    ]
    args = [sink.reshape(A_HEADS), lam_vecs, subln_g.reshape(1, B_VDIM)] + [qkv] * 6
    if lat:
        in_specs += [
            pl.BlockSpec((SEQ, 256), lambda b, i: (b, COL_KA // 256)),
            pl.BlockSpec((SEQ, 256), lambda b, i: (b, COL_VA // 256)),
            pl.BlockSpec((SEQ, 1024), lambda b, i: (b, COL_KB // 1024)),
            pl.BlockSpec((SEQ, 1024), lambda b, i: (b, COL_VB // 1024)),
        ]
        args += [qkv] * 4
    rows = NL if lat else NC
    out = jax.ShapeDtypeStruct((rows, 1024), BF16)
    out_spec = pl.BlockSpec((tq, 1024), lambda b, i: (b * nq + i, 0))
    return pl.pallas_call(
        functools.partial(_mixers_kernel, lat=lat, lam_init=lam_init),
        out_shape=(out, out),
        grid=(B, nq),
        in_specs=in_specs,
        out_specs=(out_spec, out_spec),
        compiler_params=_cparams(("arbitrary", "arbitrary")),
        name="mixers_lat" if lat else "mixers_ctx",
    )(*args)


OUT_TM = 512


def _outproj_kernel(x_ref, w_ref, g_ref, gate_ref, oal_ref, obl_ref, *rest, has_ctx):
    if has_ctx:
        oac_ref, obc_ref, o_ref, wb_scr = rest
    else:
        o_ref, wb_scr = rest
    i = pl.program_id(0)

    @pl.when(i == 0)
    def _():
        wb_scr[...] = w_ref[0].astype(BF16)

    def project(oa_ref, ob_ref):
        y = (jnp.dot(oa_ref[...], wb_scr[0:1024, :], preferred_element_type=F32)
             + jnp.dot(ob_ref[...], wb_scr[1024:2048, :], preferred_element_type=F32))
        o_ref[...] = x_ref[...] + gate_ref[0] * (_rms(y) * g_ref[...])

    if has_ctx:
        n_lat = NL // OUT_TM

        @pl.when(i < n_lat)
        def _():
            project(oal_ref, obl_ref)

        @pl.when(i >= n_lat)
        def _():
            project(oac_ref, obc_ref)
    else:
        project(oal_ref, obl_ref)


def _out_proj(x, attn_lat, attn_ctx, w_out, layer, g, mod, rows):
    tm = OUT_TM
    n_lat = NL // tm
    has_ctx = attn_ctx is not None
    lat_spec = pl.BlockSpec((tm, 1024), lambda i: (jnp.minimum(i, n_lat - 1), 0))
    ctx_spec = pl.BlockSpec((tm, 1024), lambda i: (jnp.maximum(i - n_lat, 0), 0))
    in_specs = [
        pl.BlockSpec((tm, D), lambda i: (i, 0)),
        pl.BlockSpec((1, D, D), lambda i: (layer, 0, 0), pipeline_mode=pl.Buffered(1)),
        pl.BlockSpec((1, D), lambda i: (0, 0)),
        _mod_spec(tm, 2),
        lat_spec, lat_spec,
    ]
    args = [x, w_out, g.reshape(1, D), mod, *attn_lat]
    if has_ctx:
        in_specs += [ctx_spec, ctx_spec]
        args += list(attn_ctx)
    return pl.pallas_call(
        functools.partial(_outproj_kernel, has_ctx=has_ctx),
        out_shape=jax.ShapeDtypeStruct((rows, D), F32),
        grid=(rows // tm,),
        in_specs=in_specs,
        out_specs=pl.BlockSpec((tm, D), lambda i: (i, 0)),
        scratch_shapes=[pltpu.VMEM((D, D), BF16)],
        compiler_params=_cparams(("arbitrary",)),
        name="out_proj",
    )(*args)


FFN_TM = 1024
FFN_TF = 256
FFN_HALF = 512


def _swiglu_step(hb, wg, wu, wd):
    gt = jnp.dot(hb, wg, preferred_element_type=F32)
    up = jnp.dot(hb, wu, preferred_element_type=F32)
    act = (gt * jax.nn.sigmoid(gt) * up).astype(BF16)
    return jnp.dot(act, wd, preferred_element_type=F32)


def _ffn_kernel(x_ref, gpre_ref, sh_ref, sc_ref, gate_ref, gpost_ref, wg_ref, wu_ref, wd_ref, o_ref,
                h_scr, wgb, wub, wdb):
    j = pl.program_id(1)

    @pl.when(j == 0)
    def _():
        h_scr[...] = _norm_mod(x_ref[...], gpre_ref[...], sh_ref[0], sc_ref[0]).astype(BF16)
        o_ref[...] = jnp.zeros_like(o_ref)

    wgb[...] = wg_ref[0].astype(BF16)
    wub[...] = wu_ref[0].astype(BF16)
    wdb[...] = wd_ref[0].astype(BF16)
    for r in range(FFN_TM // FFN_HALF):
        rows = slice(r * FFN_HALF, (r + 1) * FFN_HALF)
        o_ref[rows, :] += _swiglu_step(h_scr[rows, :], wgb[...], wub[...], wdb[...])

    @pl.when(j == pl.num_programs(1) - 1)
    def _():
        o_ref[...] = x_ref[...] + gate_ref[0] * (_rms(o_ref[...]) * gpost_ref[...])


def _ffn_dense(x, gpre, gpost, mod, wg, wu, wd, layer, rows):
    tm, tf = FFN_TM, FFN_TF
    return pl.pallas_call(
        _ffn_kernel,
        out_shape=jax.ShapeDtypeStruct((rows, D), F32),
        grid=(rows // tm, FF // tf),
        in_specs=[
            pl.BlockSpec((tm, D), lambda i, j: (i, 0), pipeline_mode=pl.Buffered(1)),
            pl.BlockSpec((1, D), lambda i, j: (0, 0)),
            _mod_spec(tm, 3, 2),
            _mod_spec(tm, 4, 2),
            _mod_spec(tm, 5, 2),
            pl.BlockSpec((1, D), lambda i, j: (0, 0)),
            pl.BlockSpec((1, D, tf), lambda i, j: (layer, 0, j)),
            pl.BlockSpec((1, D, tf), lambda i, j: (layer, 0, j)),
            pl.BlockSpec((1, tf, D), lambda i, j: (layer, j, 0)),
        ],
        out_specs=pl.BlockSpec((tm, D), lambda i, j: (i, 0)),
        scratch_shapes=[pltpu.VMEM((tm, D), BF16), pltpu.VMEM((D, tf), BF16), pltpu.VMEM((D, tf), BF16),
                        pltpu.VMEM((tf, D), BF16)],
        compiler_params=_cparams(("arbitrary", "arbitrary")),
        name="ffn_dense",
    )(x, gpre.reshape(1, D), mod, mod, mod, gpost.reshape(1, D), wg, wu, wd)


RT_TM = 512
MOE_SUB = 256
MOE_RS = 2560
MOE_TF = 256


def _router_kernel(x_ref, gpre_ref, sh_ref, sc_ref, wr_ref, h_ref, ri_ref, rw_ref):
    h = _norm_mod(x_ref[...], gpre_ref[...], sh_ref[0], sc_ref[0])
    h_ref[...] = h
    h_hi, h_lo = _split_bf16(h)
    w_hi, w_lo = _split_bf16(wr_ref[...])
    logits = (jnp.dot(h_hi, w_hi, preferred_element_type=F32) + jnp.dot(h_lo, w_hi, preferred_element_type=F32)
              + jnp.dot(h_hi, w_lo, preferred_element_type=F32))
    lane = lax.broadcasted_iota(jnp.int32, logits.shape, 1)
    logits = jnp.where(lane < NE, logits, -jnp.inf)
    m1 = jnp.max(logits, axis=-1, keepdims=True)
    i1 = jnp.min(jnp.where(logits == m1, lane, LANES), axis=-1, keepdims=True)
    rest = jnp.where(lane == i1, -jnp.inf, logits)
    m2 = jnp.max(rest, axis=-1, keepdims=True)
    i2 = jnp.min(jnp.where(rest == m2, lane, LANES), axis=-1, keepdims=True)
    e2 = jnp.exp(m2 - m1)
    den = 1.0 + e2
    ri_ref[...] = jnp.where(lane == 0, i1, jnp.where(lane == 1, i2, 0))
    rw_ref[...] = jnp.where(lane == 0, 1.0 / den, jnp.where(lane == 1, e2 / den, 0.0))


def _router(x, gpre, mod, wr_pad, rows):
    tm = RT_TM
    return pl.pallas_call(
        _router_kernel,
        out_shape=(jax.ShapeDtypeStruct((rows, D), F32),
                   jax.ShapeDtypeStruct((rows, LANES), jnp.int32),
                   jax.ShapeDtypeStruct((rows, LANES), F32)),
        grid=(rows // tm,),
        in_specs=[
            pl.BlockSpec((tm, D), lambda i: (i, 0)),
            pl.BlockSpec((1, D), lambda i: (0, 0)),
            _mod_spec(tm, 3),
            _mod_spec(tm, 4),
            pl.BlockSpec((D, LANES), lambda i: (0, 0)),
        ],
        out_specs=(pl.BlockSpec((tm, D), lambda i: (i, 0)),
                   pl.BlockSpec((tm, LANES), lambda i: (i, 0)),
                   pl.BlockSpec((tm, LANES), lambda i: (i, 0))),
        compiler_params=_cparams(("arbitrary",)),
        name="moe_router",
    )(x, gpre.reshape(1, D), mod, mod, wr_pad)


def _moe_kernel(te_ref, nsub_ref, src_ref, dst_ref, h_hbm, wg_ref, wu_ref, wd_ref, y_hbm,
                x_scr, stage, acc_scr, wgb, wub, wdb, sem_in, sem_out):
    s = pl.program_id(0)
    j = pl.program_id(1)
    sub = MOE_SUB
    nsub = nsub_ref[s]
    active = nsub > 0

    def rows_of(c):
        return pl.ds(pl.multiple_of(c * sub, sub), sub)

    @pl.when(jnp.logical_and(s == 0, j == 0))
    def _():
        stage[0] = jnp.zeros((sub, D), F32)
        dump = pltpu.make_async_copy(stage.at[0], y_hbm.at[pl.ds(y_hbm.shape[0] - sub, sub)], sem_out)
        dump.start()
        dump.wait()

    @pl.when(jnp.logical_and(active, j == 0))
    def _():
        def issue(c, slot):
            def one(r, carry):
                pltpu.make_async_copy(h_hbm.at[pl.ds(src_ref[0, 0, c * sub + r], 1)],
                                      stage.at[slot, pl.ds(r, 1)], sem_in.at[slot]).start()
                return carry
            lax.fori_loop(0, sub, one, 0, unroll=8)

        issue(0, 0)

        @pl.when(nsub > 1)
        def _():
            issue(1, 1)

        def chunk(c, slot):
            @pl.when(c + 2 < nsub)
            def _():
                issue(c + 2, jnp.where(slot == 0, 2, slot - 1))

            pltpu.make_async_copy(h_hbm.at[pl.ds(0, sub)], stage.at[slot], sem_in.at[slot]).wait()
            x_scr[rows_of(c)] = stage[slot].astype(BF16)
            acc_scr[rows_of(c)] = jnp.zeros((sub, D), F32)
            return jnp.where(slot == 2, 0, slot + 1)
        lax.fori_loop(0, nsub, chunk, 0)

    @pl.when(active)
    def _():
        wgb[...] = wg_ref[0, 0].astype(BF16)
        wub[...] = wu_ref[0, 0].astype(BF16)
        wdb[...] = wd_ref[0, 0].astype(BF16)

        def block(start, n_chunks):
            rows = pl.ds(pl.multiple_of(start * sub, sub), n_chunks * sub)
            acc_scr[rows] += _swiglu_step(x_scr[rows], wgb[...], wub[...], wdb[...])

        def quad(q, carry):
            block(q * 4, 2)
            block(q * 4 + 2, 2)
            return carry
        lax.fori_loop(0, nsub >> 2, quad, 0)

        @pl.when((nsub & 2) == 2)
        def _():
            block((nsub >> 2) * 4, 2)

        @pl.when((nsub & 1) == 1)
        def _():
            block(nsub - 1, 1)

    @pl.when(jnp.logical_and(active, j == pl.num_programs(1) - 1))
    def _():
        def issue(g, carry):
            base = pl.multiple_of(g * 8, 8)
            for u in range(8):
                pltpu.make_async_copy(acc_scr.at[pl.ds(base + u, 1)],
                                      y_hbm.at[pl.ds(dst_ref[0, 0, base + u], 1)], sem_out).start()
            return carry
        lax.fori_loop(0, nsub * (sub // 8), issue, 0)

        def drain(c, carry):
            pltpu.make_async_copy(acc_scr.at[pl.ds(0, sub)], y_hbm.at[pl.ds(0, sub)], sem_out).wait()
            return carry
        lax.fori_loop(0, nsub, drain, 0)


def _moe_plan(ri, rows):
    sub, rs = MOE_SUB, MOE_RS
    n_asg = 2 * rows
    ns_max = (n_asg + NE * (sub - 1)) // rs + NE
    e_flat = jnp.concatenate([ri[:, 0], ri[:, 1]])
    _, order = lax.sort((e_flat, jnp.arange(n_asg, dtype=jnp.int32)), num_keys=1)
    counts = jnp.sum((e_flat[:, None] == jnp.arange(NE, dtype=jnp.int32)[None, :]).astype(jnp.int32), axis=0)
    ustart = jnp.cumsum(counts) - counts
    psz = ((counts + sub - 1) // sub) * sub
    nst = (psz + rs - 1) // rs
    st_end = jnp.cumsum(nst)
    total = st_end[NE - 1]
    s_idx = jnp.arange(ns_max, dtype=jnp.int32)
    valid = s_idx < total
    s_eff = jnp.minimum(s_idx, total - 1)
    e_s = jnp.sum((s_eff[:, None] >= st_end[None, :]).astype(jnp.int32), axis=1)
    k_s = s_eff - (st_end - nst)[e_s]
    nsub = jnp.where(valid, jnp.clip(psz[e_s] - k_s * rs, 0, rs) // sub, 0)
    lane = jnp.arange(rs, dtype=jnp.int32)[None, :]
    n_real = jnp.where(valid, jnp.clip(counts[e_s] - k_s * rs, 0, rs), 0)
    row_valid = lane < n_real[:, None]
    base = ustart[e_s] + k_s * rs
    order_pad = jnp.concatenate([order, jnp.zeros((rs,), jnp.int32)])
    asg = jnp.stack([lax.dynamic_slice(order_pad, (base[s],), (rs,)) for s in range(ns_max)])
    src = jnp.where(row_valid, asg % rows, 0)
    dst = jnp.where(row_valid, asg, n_asg + lane % sub)
    plan = (e_s.astype(jnp.int32), nsub.astype(jnp.int32),
            src.astype(jnp.int32).reshape(ns_max, 1, rs), dst.astype(jnp.int32).reshape(ns_max, 1, rs))
    return plan, total


def _moe_experts(h, plan, wg, wu, wd, layer, rows):
    sub, rs, tf = MOE_SUB, MOE_RS, MOE_TF
    tile_e, nsub, src, dst = plan
    ns_max = src.shape[0]
    nj = FF // tf
    jmap = lambda j, ns, s: jnp.where(ns[s] > 0, j, nj - 1)
    grid_spec = pltpu.PrefetchScalarGridSpec(
        num_scalar_prefetch=2,
        grid=(ns_max, nj),
        in_specs=[
            pl.BlockSpec((1, 1, rs), lambda s, j, te, ns: (s, 0, 0), memory_space=pltpu.SMEM),
            pl.BlockSpec((1, 1, rs), lambda s, j, te, ns: (s, 0, 0), memory_space=pltpu.SMEM),
            pl.BlockSpec(memory_space=pl.ANY),
            pl.BlockSpec((1, 1, D, tf), lambda s, j, te, ns: (layer, te[s], 0, jmap(j, ns, s))),
            pl.BlockSpec((1, 1, D, tf), lambda s, j, te, ns: (layer, te[s], 0, jmap(j, ns, s))),
            pl.BlockSpec((1, 1, tf, D), lambda s, j, te, ns: (layer, te[s], jmap(j, ns, s), 0)),
        ],
        out_specs=pl.BlockSpec(memory_space=pl.ANY),
        scratch_shapes=[
            pltpu.VMEM((rs, D), BF16),
            pltpu.VMEM((3, sub, D), F32),
            pltpu.VMEM((rs, D), F32),
            pltpu.VMEM((D, tf), BF16),
            pltpu.VMEM((D, tf), BF16),
            pltpu.VMEM((tf, D), BF16),
            pltpu.SemaphoreType.DMA((3,)),
            pltpu.SemaphoreType.DMA(()),
        ],
    )
    return pl.pallas_call(
        _moe_kernel,
        out_shape=jax.ShapeDtypeStruct((2 * rows + sub, D), F32),
        grid_spec=grid_spec,
        compiler_params=_cparams(("arbitrary", "arbitrary")),
        name="moe_experts",
    )(tile_e, nsub, src, dst, h, wg, wu, wd)


POST_TM = 512


def _moe_post_kernel(x_ref, y0_ref, y1_ref, rw_ref, gpost_ref, gate_ref, o_ref):
    w = rw_ref[...]
    f = w[:, 0:1] * y0_ref[...] + w[:, 1:2] * y1_ref[...]
    o_ref[...] = x_ref[...] + gate_ref[0] * (_rms(f) * gpost_ref[...])


def _moe_post(x, y2, rw, gpost, mod, rows):
    tm = POST_TM
    off = rows // tm
    return pl.pallas_call(
        _moe_post_kernel,
        out_shape=jax.ShapeDtypeStruct((rows, D), F32),
        grid=(rows // tm,),
        in_specs=[
            pl.BlockSpec((tm, D), lambda i: (i, 0)),
            pl.BlockSpec((tm, D), lambda i: (i, 0)),
            pl.BlockSpec((tm, D), lambda i: (off + i, 0)),
            pl.BlockSpec((tm, LANES), lambda i: (i, 0)),
            pl.BlockSpec((1, D), lambda i: (0, 0)),
            _mod_spec(tm, 5),
        ],
        out_specs=pl.BlockSpec((tm, D), lambda i: (i, 0)),
        compiler_params=_cparams(("arbitrary",)),
        name="moe_post",
    )(x, y2, y2, rw, gpost.reshape(1, D), mod)


def _rope_tables():
    rows = SEQ // GRID_W
    row = jnp.repeat(jnp.arange(rows), GRID_W).astype(F32)
    col = jnp.tile(jnp.arange(GRID_W), rows).astype(F32)
    inv = ROPE_BASE ** (-jnp.arange(ROPE_PAIRS, dtype=F32) / ROPE_PAIRS)
    ang_r = row[:, None] * inv[None, :]
    ang_c = col[:, None] * inv[None, :]
    cr, sr, cc, sc = jnp.cos(ang_r), jnp.sin(ang_r), jnp.cos(ang_c), jnp.sin(ang_c)
    cos_t = jnp.concatenate([cr, cr, cc, cc], axis=-1)
    sin_t = jnp.concatenate([-sr, sr, -sc, sc], axis=-1)
    cos_t = jnp.concatenate([cos_t, jnp.ones((IN_TM, LANES), F32)], axis=0)
    sin_t = jnp.concatenate([sin_t, jnp.zeros((IN_TM, LANES), F32)], axis=0)
    return cos_t, sin_t


def kernel(x, c, ctx, c_ctx, w_mod, b_mod, g_attn_pre, g_attn_post, g_ffn_pre, g_ffn_post, w_in, w_out,
           sink_logit, lambda_q1, lambda_k1, lambda_q2, lambda_k2, subln_g, ffn_w_gate, ffn_w_up,
           ffn_w_down, moe_router, moe_w_gate, moe_w_up, moe_w_down):
    xs = jnp.concatenate([x.reshape(NL, D), ctx.reshape(NC, D)], axis=0)
    cin = jnp.concatenate([c, c_ctx[None, :], jnp.zeros((8 - B - 1, D), F32)], axis=0)
    mod_all = _modulation(cin, w_mod, b_mod)
    cos_t, sin_t = _rope_tables()
    w_in_bf = w_in.astype(BF16)

    for l in range(DEPTH):
        last = l == DEPTH - 1
        rows = NL if last else NR
        mod = mod_all[l].reshape(8, 1, N_MOD * D)
        lam_init = 0.8 - 0.6 * math.exp(-0.3 * l)
        lam_vecs = jnp.stack([lambda_q1[l], lambda_k1[l], lambda_q2[l], lambda_k2[l]])

        qkv = _in_proj(xs, g_attn_pre[l], mod, w_in_bf, l, cos_t, sin_t)
        attn_lat = _mixers(qkv, sink_logit[l], lam_vecs, subln_g[l], lam_init, lat=True)
        attn_ctx = None if last else _mixers(qkv, sink_logit[l], lam_vecs, subln_g[l], lam_init, lat=False)
        xs = _out_proj(xs, attn_lat, attn_ctx, w_out, l, g_attn_post[l], mod, rows)

        i = l // 2
        if l % 2 == 0:
            xs = _ffn_dense(xs, g_ffn_pre[l], g_ffn_post[l], mod, ffn_w_gate, ffn_w_up, ffn_w_down, i, rows)
        else:
            wr_pad = jnp.pad(moe_router[i], ((0, 0), (0, LANES - NE)))
            h, ri, rw = _router(xs, g_ffn_pre[l], mod, wr_pad, rows)
            plan, n_super = _moe_plan(ri, rows)

            def experts(n, plan=plan, h=h, i=i, rows=rows):
                return _moe_experts(h, tuple(p[:n] for p in plan), moe_w_gate, moe_w_up, moe_w_down, i, rows)

            y2 = lax.cond(n_super <= NE, functools.partial(experts, NE),
                          functools.partial(experts, plan[0].shape[0]))
            xs = _moe_post(xs, y2, rw, g_ffn_post[l], mod, rows)
    return xs.reshape(B, SEQ, D)
```

```python
import functools
import math

import jax
import jax.numpy as jnp
from jax import lax
from jax.experimental import pallas as pl
from jax.experimental.pallas import tpu as pltpu

D = 2048
B = 4
SEQ = 2048
DEPTH = 4
GRID_W = 64
CTX = 256
HD = 128
WINDOW = 128
A_HEADS = 8
A_KVH = 2
A_GROUP = 4
B_HEADS = 4
B_VDIM = 256
FF = 5632
NE = 8
N_MOD = 6
ROPE_BASE = 10000.0
ROPE_PAIRS = 32
ATTN_SCALE = HD ** -0.5
EPS = 1e-6
NEG = -1e30

NL = B * SEQ
NC = B * CTX
NR = NL + NC
IN_COLS = 4608
COL_QA, COL_QB, COL_KB, COL_VB, COL_KA, COL_VA = 0, 1024, 2048, 3072, 4096, 4352

LANES = 128
VMEM_LIMIT = 56 * 1024 * 1024

F32 = jnp.float32
BF16 = jnp.bfloat16


def _cparams(sem):
    return pltpu.CompilerParams(dimension_semantics=sem, vmem_limit_bytes=VMEM_LIMIT)


def _nt_dot(a, b):
    return lax.dot_general(a, b, (((1,), (1,)), ((), ())), preferred_element_type=F32)


def _rms(x):
    return x * lax.rsqrt(jnp.mean(x * x, axis=-1, keepdims=True) + EPS)


def _norm_mod(x, g, sh, sc):
    return (_rms(x) * g) * (1.0 + sc) + sh


def _mod_row(i, tm):
    return jnp.where(i < NL // tm, (i * tm) // SEQ, B)


def _mod_spec(tm, chunk, grid_rank=1):
    if grid_rank == 1:
        return pl.BlockSpec((1, 1, D), lambda i: (_mod_row(i, tm), 0, chunk))
    return pl.BlockSpec((1, 1, D), lambda i, j: (_mod_row(i, tm), 0, chunk))


MOD_TN = 1024


def _split_bf16(v):
    hi = v.astype(BF16)
    return hi, (v - hi.astype(F32)).astype(BF16)


def _mod_kernel(c_ref, w_ref, b_ref, o_ref):
    c = c_ref[...]
    a_hi, a_lo = _split_bf16(c * jax.nn.sigmoid(c))
    w_hi, w_lo = _split_bf16(w_ref[0])
    both = jnp.dot(jnp.concatenate([a_hi, a_lo], axis=0), w_hi, preferred_element_type=F32)
    o_ref[0] = both[0:8] + both[8:16] + jnp.dot(a_hi, w_lo, preferred_element_type=F32) + b_ref[0]


def _modulation(cin, w_mod, b_mod):
    return pl.pallas_call(
        _mod_kernel,
        out_shape=jax.ShapeDtypeStruct((DEPTH, 8, N_MOD * D), F32),
        grid=(DEPTH, N_MOD * D // MOD_TN),
        in_specs=[
            pl.BlockSpec((8, D), lambda l, n: (0, 0)),
            pl.BlockSpec((1, D, MOD_TN), lambda l, n: (l, 0, n)),
            pl.BlockSpec((1, 1, MOD_TN), lambda l, n: (l, 0, n)),
        ],
        out_specs=pl.BlockSpec((1, 8, MOD_TN), lambda l, n: (l, 0, n)),
        compiler_params=_cparams(("arbitrary", "arbitrary")),
        name="modulation",
    )(cin, w_mod, b_mod.reshape(DEPTH, 1, N_MOD * D))


IN_TM = 256
IN_CH = 512
LOG2E = math.log2(math.e)
Q_SCALE = ATTN_SCALE * LOG2E
IN_CHUNKS = ((0, 4, True), (1, 4, True), (8, 2, False), (2, 4, True), (3, 4, True),
             (4, 4, False), (5, 4, False), (6, 0, False), (7, 0, False))


def _inproj_kernel(x_ref, g_ref, sh_ref, sc_ref, cos_ref, sin_ref, w_ref, o_ref):
    hb = _norm_mod(x_ref[...], g_ref[...], sh_ref[0], sc_ref[0]).astype(BF16)
    cos = cos_ref[...]
    sin = sin_ref[...]
    lane = lax.broadcasted_iota(jnp.int32, (IN_TM, LANES), 1)
    first = (lane & 63) < 32
    for c, (dest, n_rope, is_q) in enumerate(IN_CHUNKS):
        acc = jnp.dot(hb, w_ref[0, :, c * IN_CH:(c + 1) * IN_CH], preferred_element_type=F32)
        for s in range(IN_CH // LANES):
            y = acc[:, s * LANES:(s + 1) * LANES]
            if s < n_rope:
                partner = jnp.where(first, pltpu.roll(y, 96, 1), pltpu.roll(y, 32, 1))
                y = y * cos + partner * sin
                if is_q:
                    y = y * Q_SCALE
            col = dest * IN_CH + s * LANES
            o_ref[:, col:col + LANES] = y.astype(BF16)


def _in_proj(x, g, mod, w_bf, layer, cos_t, sin_t):
    rows = x.shape[0]
    tm = IN_TM
    tab = lambda i: (jnp.where(i < NL // tm, i % (SEQ // tm), SEQ // tm), 0)
    return pl.pallas_call(
        _inproj_kernel,
        out_shape=jax.ShapeDtypeStruct((rows, IN_COLS), BF16),
        grid=(rows // tm,),
        in_specs=[
            pl.BlockSpec((tm, D), lambda i: (i, 0)),
            pl.BlockSpec((1, D), lambda i: (0, 0)),
            _mod_spec(tm, 0),
            _mod_spec(tm, 1),
            pl.BlockSpec((tm, LANES), tab),
            pl.BlockSpec((tm, LANES), tab),
            pl.BlockSpec((1, D, IN_COLS), lambda i: (layer, 0, 0)),
        ],
        out_specs=pl.BlockSpec((tm, IN_COLS), lambda i: (i, 0)),
        compiler_params=_cparams(("arbitrary",)),
        name="in_proj",
    )(x, g.reshape(1, D), mod, mod, cos_t, sin_t, w_bf)


ATT_TQ_LAT = 256
ATT_TQ_CTX = 256


def _attn_a_body(sink_ref, q_ref, kc_ref, vc_ref, kl_ref, vl_ref, o_ref, local, kv_heads):
    tq = q_ref.shape[0]
    a_win = tq + 2 * WINDOW
    i = pl.program_id(1)
    if local:
        st = pl.multiple_of(jnp.clip(i * tq - WINDOW, 0, SEQ - a_win), WINDOW)
        qpos = i * tq + lax.broadcasted_iota(jnp.int32, (tq, a_win), 0)
        kpos = st + lax.broadcasted_iota(jnp.int32, (tq, a_win), 1)
        band = jnp.where(jnp.abs(qpos - kpos) <= WINDOW, 1, 0)
        band = jnp.concatenate([band] * A_GROUP, axis=0)
    for kv in kv_heads:
        ksl = slice(kv * HD, (kv + 1) * HD)
        qs = jnp.concatenate(
            [q_ref[:, (kv * A_GROUP + g) * HD:(kv * A_GROUP + g + 1) * HD] for g in range(A_GROUP)], axis=0)
        sink = jnp.concatenate(
            [jnp.full((tq, 1), sink_ref[kv * A_GROUP + g] * LOG2E, F32) for g in range(A_GROUP)], axis=0)
        s_ctx = _nt_dot(qs, kc_ref[:, ksl])
        m = jnp.maximum(jnp.max(s_ctx, axis=-1, keepdims=True), sink)
        if local:
            s_loc = jnp.where(band > 0, _nt_dot(qs, kl_ref[pl.ds(st, a_win), ksl]), NEG)
            m = jnp.maximum(m, jnp.max(s_loc, axis=-1, keepdims=True))
        p_ctx = jnp.exp2(s_ctx - m)
        den = jnp.sum(p_ctx, axis=-1, keepdims=True) + jnp.exp2(sink - m)
        o = jnp.dot(p_ctx.astype(BF16), vc_ref[:, ksl], preferred_element_type=F32)
        if local:
            p_loc = jnp.exp2(s_loc - m)
            den = den + jnp.sum(p_loc, axis=-1, keepdims=True)
            o = o + jnp.dot(p_loc.astype(BF16), vl_ref[pl.ds(st, a_win), ksl], preferred_element_type=F32)
        o = o * (1.0 / den)
        for g in range(A_GROUP):
            h = kv * A_GROUP + g
            o_ref[:, h * HD:(h + 1) * HD] = o[g * tq:(g + 1) * tq].astype(BF16)


def _attn_b_body(lam_ref, g_ref, q_ref, kc_ref, vc_ref, kl_ref, vl_ref, o_ref, with_lat, lam_init, heads):
    lv = lam_ref[...]
    lam = (jnp.exp(jnp.sum(lv[0:1] * lv[1:2], axis=-1, keepdims=True))
           - jnp.exp(jnp.sum(lv[2:3] * lv[3:4], axis=-1, keepdims=True)) + lam_init)
    for h in heads:
        vsl = slice(h * B_VDIM, (h + 1) * B_VDIM)
        pv, rden, scores = [], [], []
        for mth in range(2):
            sl = slice((2 * h + mth) * HD, (2 * h + mth + 1) * HD)
            q = q_ref[:, sl]
            scores.append((_nt_dot(q, kc_ref[:, sl]), _nt_dot(q, kl_ref[:, sl]) if with_lat else None))
        for mth in range(2):
            s_c, s_l = scores[mth]
            mx = jnp.max(s_c, axis=-1, keepdims=True)
            if with_lat:
                mx = jnp.maximum(mx, jnp.max(s_l, axis=-1, keepdims=True))
            e_c = jnp.exp2(s_c - mx)
            den = jnp.sum(e_c, axis=-1, keepdims=True)
            o_m = jnp.dot(e_c.astype(BF16), vc_ref[:, vsl], preferred_element_type=F32)
            if with_lat:
                e_l = jnp.exp2(s_l - mx)
                den = den + jnp.sum(e_l, axis=-1, keepdims=True)
                o_m = o_m + jnp.dot(e_l.astype(BF16), vl_ref[:, vsl], preferred_element_type=F32)
            pv.append(o_m)
            rden.append(1.0 / den)
        o = pv[0] * rden[0] - pv[1] * (lam * rden[1])
        o = (_rms(o) * g_ref[...]) * (1.0 - lam_init)
        o_ref[:, vsl] = o.astype(BF16)


def _mixers_kernel(sink_ref, lam_ref, g_ref, qa_ref, kac_ref, vac_ref, qb_ref, kbc_ref, vbc_ref, *rest,
                   lat, lam_init):
    if lat:
        kal_ref, val_ref, kbl_ref, vbl_ref, oa_ref, ob_ref = rest
    else:
        oa_ref, ob_ref = rest
        kal_ref = val_ref = kbl_ref = vbl_ref = None
    mixer_a = functools.partial(_attn_a_body, sink_ref, qa_ref, kac_ref, vac_ref, kal_ref, val_ref, oa_ref, lat)
    mixer_b = functools.partial(_attn_b_body, lam_ref, g_ref, qb_ref, kbc_ref, vbc_ref, kbl_ref, vbl_ref, ob_ref,
                                lat, lam_init)
    mixer_a(range(A_KVH))
    mixer_b(range(B_HEADS))


def _mixers(qkv, sink, lam_vecs, subln_g, lam_init, lat):
    tq = ATT_TQ_LAT if lat else ATT_TQ_CTX
    nq = SEQ // tq if lat else CTX // tq
    row0 = 0 if lat else NL // tq
    ctx0 = NL // CTX
    qrow = lambda b, i: row0 + b * nq + i
    in_specs = [
        pl.BlockSpec(memory_space=pltpu.SMEM),
        pl.BlockSpec((4, HD), lambda b, i: (0, 0)),
        pl.BlockSpec((1, B_VDIM), lambda b, i: (0, 0)),
        pl.BlockSpec((tq, 1024), lambda b, i: (qrow(b, i), COL_QA // 1024)),
        pl.BlockSpec((CTX, 256), lambda b, i: (ctx0 + b, COL_KA // 256)),
        pl.BlockSpec((CTX, 256), lambda b, i: (ctx0 + b, COL_VA // 256)),
        pl.BlockSpec((tq, 1024), lambda b, i: (qrow(b, i), COL_QB // 1024)),
        pl.BlockSpec((CTX, 1024), lambda b, i: (ctx0 + b, COL_KB // 1024)),
        pl.BlockSpec((CTX, 1024), lambda b, i: (ctx0 + b, COL_VB // 1024)),
    ]
    args = [sink.reshape(A_HEADS), lam_vecs, subln_g.reshape(1, B_VDIM)] + [qkv] * 6
    if lat:
        in_specs += [
            pl.BlockSpec((SEQ, 256), lambda b, i: (b, COL_KA // 256)),
            pl.BlockSpec((SEQ, 256), lambda b, i: (b, COL_VA // 256)),
            pl.BlockSpec((SEQ, 1024), lambda b, i: (b, COL_KB // 1024)),
            pl.BlockSpec((SEQ, 1024), lambda b, i: (b, COL_VB // 1024)),
        ]
        args += [qkv] * 4
    rows = NL if lat else NC
    out = jax.ShapeDtypeStruct((rows, 1024), BF16)
    out_spec = pl.BlockSpec((tq, 1024), lambda b, i: (b * nq + i, 0))
    return pl.pallas_call(
        functools.partial(_mixers_kernel, lat=lat, lam_init=lam_init),
        out_shape=(out, out),
        grid=(B, nq),
        in_specs=in_specs,
        out_specs=(out_spec, out_spec),
        compiler_params=_cparams(("arbitrary", "arbitrary")),
        name="mixers_lat" if lat else "mixers_ctx",
    )(*args)


OUT_TM = 512


def _outproj_kernel(x_ref, w_ref, g_ref, gate_ref, oal_ref, obl_ref, *rest, has_ctx):
    if has_ctx:
        oac_ref, obc_ref, o_ref, wb_scr = rest
    else:
        o_ref, wb_scr = rest
    i = pl.program_id(0)

    @pl.when(i == 0)
    def _():
        wb_scr[...] = w_ref[0].astype(BF16)

    def project(oa_ref, ob_ref):
        y = (jnp.dot(oa_ref[...], wb_scr[0:1024, :], preferred_element_type=F32)
             + jnp.dot(ob_ref[...], wb_scr[1024:2048, :], preferred_element_type=F32))
        o_ref[...] = x_ref[...] + gate_ref[0] * (_rms(y) * g_ref[...])

    if has_ctx:
        n_lat = NL // OUT_TM

        @pl.when(i < n_lat)
        def _():
            project(oal_ref, obl_ref)

        @pl.when(i >= n_lat)
        def _():
            project(oac_ref, obc_ref)
    else:
        project(oal_ref, obl_ref)


def _out_proj(x, attn_lat, attn_ctx, w_out, layer, g, mod, rows):
    tm = OUT_TM
    n_lat = NL // tm
    has_ctx = attn_ctx is not None
    lat_spec = pl.BlockSpec((tm, 1024), lambda i: (jnp.minimum(i, n_lat - 1), 0))
    ctx_spec = pl.BlockSpec((tm, 1024), lambda i: (jnp.maximum(i - n_lat, 0), 0))
    in_specs = [
        pl.BlockSpec((tm, D), lambda i: (i, 0)),
        pl.BlockSpec((1, D, D), lambda i: (layer, 0, 0), pipeline_mode=pl.Buffered(1)),
        pl.BlockSpec((1, D), lambda i: (0, 0)),
        _mod_spec(tm, 2),
        lat_spec, lat_spec,
    ]
    args = [x, w_out, g.reshape(1, D), mod, *attn_lat]
    if has_ctx:
        in_specs += [ctx_spec, ctx_spec]
        args += list(attn_ctx)
    return pl.pallas_call(
        functools.partial(_outproj_kernel, has_ctx=has_ctx),
        out_shape=jax.ShapeDtypeStruct((rows, D), F32),
        grid=(rows // tm,),
        in_specs=in_specs,
        out_specs=pl.BlockSpec((tm, D), lambda i: (i, 0)),
        scratch_shapes=[pltpu.VMEM((D, D), BF16)],
        compiler_params=_cparams(("arbitrary",)),
        name="out_proj",
    )(*args)


FFN_TM = 1024
FFN_TF = 256
FFN_HALF = 512


def _swiglu_step(hb, wg, wu, wd):
    gt = jnp.dot(hb, wg, preferred_element_type=F32)
    up = jnp.dot(hb, wu, preferred_element_type=F32)
    act = (gt * jax.nn.sigmoid(gt) * up).astype(BF16)
    return jnp.dot(act, wd, preferred_element_type=F32)


def _ffn_kernel(x_ref, gpre_ref, sh_ref, sc_ref, gate_ref, gpost_ref, wg_ref, wu_ref, wd_ref, o_ref,
                h_scr, wgb, wub, wdb):
    j = pl.program_id(1)

    @pl.when(j == 0)
    def _():
        h_scr[...] = _norm_mod(x_ref[...], gpre_ref[...], sh_ref[0], sc_ref[0]).astype(BF16)
        o_ref[...] = jnp.zeros_like(o_ref)

    wgb[...] = wg_ref[0].astype(BF16)
    wub[...] = wu_ref[0].astype(BF16)
    wdb[...] = wd_ref[0].astype(BF16)
    for r in range(FFN_TM // FFN_HALF):
        rows = slice(r * FFN_HALF, (r + 1) * FFN_HALF)
        o_ref[rows, :] += _swiglu_step(h_scr[rows, :], wgb[...], wub[...], wdb[...])

    @pl.when(j == pl.num_programs(1) - 1)
    def _():
        o_ref[...] = x_ref[...] + gate_ref[0] * (_rms(o_ref[...]) * gpost_ref[...])


def _ffn_dense(x, gpre, gpost, mod, wg, wu, wd, layer, rows):
    tm, tf = FFN_TM, FFN_TF
    return pl.pallas_call(
        _ffn_kernel,
        out_shape=jax.ShapeDtypeStruct((rows, D), F32),
        grid=(rows // tm, FF // tf),
        in_specs=[
            pl.BlockSpec((tm, D), lambda i, j: (i, 0), pipeline_mode=pl.Buffered(1)),
            pl.BlockSpec((1, D), lambda i, j: (0, 0)),
            _mod_spec(tm, 3, 2),
            _mod_spec(tm, 4, 2),
            _mod_spec(tm, 5, 2),
            pl.BlockSpec((1, D), lambda i, j: (0, 0)),
            pl.BlockSpec((1, D, tf), lambda i, j: (layer, 0, j)),
            pl.BlockSpec((1, D, tf), lambda i, j: (layer, 0, j)),
            pl.BlockSpec((1, tf, D), lambda i, j: (layer, j, 0)),
        ],
        out_specs=pl.BlockSpec((tm, D), lambda i, j: (i, 0)),
        scratch_shapes=[pltpu.VMEM((tm, D), BF16), pltpu.VMEM((D, tf), BF16), pltpu.VMEM((D, tf), BF16),
                        pltpu.VMEM((tf, D), BF16)],
        compiler_params=_cparams(("arbitrary", "arbitrary")),
        name="ffn_dense",
    )(x, gpre.reshape(1, D), mod, mod, mod, gpost.reshape(1, D), wg, wu, wd)


RT_TM = 512
MOE_SUB = 256
MOE_RS = 2560
MOE_TF = 256


def _router_kernel(x_ref, gpre_ref, sh_ref, sc_ref, wr_ref, h_ref, ri_ref, rw_ref):
    h = _norm_mod(x_ref[...], gpre_ref[...], sh_ref[0], sc_ref[0])
    h_ref[...] = h
    h_hi, h_lo = _split_bf16(h)
    w_hi, w_lo = _split_bf16(wr_ref[...])
    logits = (jnp.dot(h_hi, w_hi, preferred_element_type=F32) + jnp.dot(h_lo, w_hi, preferred_element_type=F32)
              + jnp.dot(h_hi, w_lo, preferred_element_type=F32))
    lane = lax.broadcasted_iota(jnp.int32, logits.shape, 1)
    logits = jnp.where(lane < NE, logits, -jnp.inf)
    m1 = jnp.max(logits, axis=-1, keepdims=True)
    i1 = jnp.min(jnp.where(logits == m1, lane, LANES), axis=-1, keepdims=True)
    rest = jnp.where(lane == i1, -jnp.inf, logits)
    m2 = jnp.max(rest, axis=-1, keepdims=True)
    i2 = jnp.min(jnp.where(rest == m2, lane, LANES), axis=-1, keepdims=True)
    e2 = jnp.exp(m2 - m1)
    den = 1.0 + e2
    ri_ref[...] = jnp.where(lane == 0, i1, jnp.where(lane == 1, i2, 0))
    rw_ref[...] = jnp.where(lane == 0, 1.0 / den, jnp.where(lane == 1, e2 / den, 0.0))


def _router(x, gpre, mod, wr_pad, rows):
    tm = RT_TM
    return pl.pallas_call(
        _router_kernel,
        out_shape=(jax.ShapeDtypeStruct((rows, D), F32),
                   jax.ShapeDtypeStruct((rows, LANES), jnp.int32),
                   jax.ShapeDtypeStruct((rows, LANES), F32)),
        grid=(rows // tm,),
        in_specs=[
            pl.BlockSpec((tm, D), lambda i: (i, 0)),
            pl.BlockSpec((1, D), lambda i: (0, 0)),
            _mod_spec(tm, 3),
            _mod_spec(tm, 4),
            pl.BlockSpec((D, LANES), lambda i: (0, 0)),
        ],
        out_specs=(pl.BlockSpec((tm, D), lambda i: (i, 0)),
                   pl.BlockSpec((tm, LANES), lambda i: (i, 0)),
                   pl.BlockSpec((tm, LANES), lambda i: (i, 0))),
        compiler_params=_cparams(("arbitrary",)),
        name="moe_router",
    )(x, gpre.reshape(1, D), mod, mod, wr_pad)


def _moe_kernel(te_ref, nsub_ref, src_ref, dst_ref, h_hbm, wg_ref, wu_ref, wd_ref, y_hbm,
                x_scr, stage, acc_scr, wgb, wub, wdb, sem_in, sem_out):
    s = pl.program_id(0)
    j = pl.program_id(1)
    sub = MOE_SUB
    nsub = nsub_ref[s]
    active = nsub > 0

    def rows_of(c):
        return pl.ds(pl.multiple_of(c * sub, sub), sub)

    @pl.when(jnp.logical_and(s == 0, j == 0))
    def _():
        stage[0] = jnp.zeros((sub, D), F32)
        dump = pltpu.make_async_copy(stage.at[0], y_hbm.at[pl.ds(y_hbm.shape[0] - sub, sub)], sem_out)
        dump.start()
        dump.wait()

    @pl.when(jnp.logical_and(active, j == 0))
    def _():
        def issue(c, slot):
            def one(r, carry):
                pltpu.make_async_copy(h_hbm.at[pl.ds(src_ref[0, 0, c * sub + r], 1)],
                                      stage.at[slot, pl.ds(r, 1)], sem_in.at[slot]).start()
                return carry
            lax.fori_loop(0, sub, one, 0, unroll=8)

        issue(0, 0)

        @pl.when(nsub > 1)
        def _():
            issue(1, 1)

        def chunk(c, slot):
            @pl.when(c + 2 < nsub)
            def _():
                issue(c + 2, jnp.where(slot == 0, 2, slot - 1))

            pltpu.make_async_copy(h_hbm.at[pl.ds(0, sub)], stage.at[slot], sem_in.at[slot]).wait()
            x_scr[rows_of(c)] = stage[slot].astype(BF16)
            acc_scr[rows_of(c)] = jnp.zeros((sub, D), F32)
            return jnp.where(slot == 2, 0, slot + 1)
        lax.fori_loop(0, nsub, chunk, 0)

    @pl.when(active)
    def _():
        wgb[...] = wg_ref[0, 0].astype(BF16)
        wub[...] = wu_ref[0, 0].astype(BF16)
        wdb[...] = wd_ref[0, 0].astype(BF16)

        def block(start, n_chunks):
            rows = pl.ds(pl.multiple_of(start * sub, sub), n_chunks * sub)
            acc_scr[rows] += _swiglu_step(x_scr[rows], wgb[...], wub[...], wdb[...])

        def quad(q, carry):
            block(q * 4, 2)
            block(q * 4 + 2, 2)
            return carry
        lax.fori_loop(0, nsub >> 2, quad, 0)

        @pl.when((nsub & 2) == 2)
        def _():
            block((nsub >> 2) * 4, 2)

        @pl.when((nsub & 1) == 1)
        def _():
            block(nsub - 1, 1)

    @pl.when(jnp.logical_and(active, j == pl.num_programs(1) - 1))
    def _():
        def issue(g, carry):
            base = pl.multiple_of(g * 8, 8)
            for u in range(8):
                pltpu.make_async_copy(acc_scr.at[pl.ds(base + u, 1)],
                                      y_hbm.at[pl.ds(dst_ref[0, 0, base + u], 1)], sem_out).start(priority=u % 2)
            return carry
        lax.fori_loop(0, nsub * (sub // 8), issue, 0)

        def drain(c, carry):
            pltpu.make_async_copy(acc_scr.at[pl.ds(0, sub)], y_hbm.at[pl.ds(0, sub)], sem_out).wait()
            return carry
        lax.fori_loop(0, nsub, drain, 0)


def _moe_plan(ri, rows):
    sub, rs = MOE_SUB, MOE_RS
    n_asg = 2 * rows
    ns_max = (n_asg + NE * (sub - 1)) // rs + NE
    e_flat = jnp.concatenate([ri[:, 0], ri[:, 1]])
    _, order = lax.sort((e_flat, jnp.arange(n_asg, dtype=jnp.int32)), num_keys=1)
    counts = jnp.sum((e_flat[:, None] == jnp.arange(NE, dtype=jnp.int32)[None, :]).astype(jnp.int32), axis=0)
    ustart = jnp.cumsum(counts) - counts
    psz = ((counts + sub - 1) // sub) * sub
    nst = (psz + rs - 1) // rs
    st_end = jnp.cumsum(nst)
    total = st_end[NE - 1]
    s_idx = jnp.arange(ns_max, dtype=jnp.int32)
    valid = s_idx < total
    s_eff = jnp.minimum(s_idx, total - 1)
    e_s = jnp.sum((s_eff[:, None] >= st_end[None, :]).astype(jnp.int32), axis=1)
    k_s = s_eff - (st_end - nst)[e_s]
    nsub = jnp.where(valid, jnp.clip(psz[e_s] - k_s * rs, 0, rs) // sub, 0)
    lane = jnp.arange(rs, dtype=jnp.int32)[None, :]
    n_real = jnp.where(valid, jnp.clip(counts[e_s] - k_s * rs, 0, rs), 0)
    row_valid = lane < n_real[:, None]
    base = ustart[e_s] + k_s * rs
    order_pad = jnp.concatenate([order, jnp.zeros((rs,), jnp.int32)])
    asg = jnp.stack([lax.dynamic_slice(order_pad, (base[s],), (rs,)) for s in range(ns_max)])
    src = jnp.where(row_valid, asg % rows, 0)
    dst = jnp.where(row_valid, asg, n_asg + lane % sub)
    plan = (e_s.astype(jnp.int32), nsub.astype(jnp.int32),
            src.astype(jnp.int32).reshape(ns_max, 1, rs), dst.astype(jnp.int32).reshape(ns_max, 1, rs))
    return plan, total


def _moe_experts(h, plan, wg, wu, wd, layer, rows):
    sub, rs, tf = MOE_SUB, MOE_RS, MOE_TF
    tile_e, nsub, src, dst = plan
    ns_max = src.shape[0]
    nj = FF // tf
    jmap = lambda j, ns, s: jnp.where(ns[s] > 0, j, nj - 1)
    grid_spec = pltpu.PrefetchScalarGridSpec(
        num_scalar_prefetch=2,
        grid=(ns_max, nj),
        in_specs=[
            pl.BlockSpec((1, 1, rs), lambda s, j, te, ns: (s, 0, 0), memory_space=pltpu.SMEM),
            pl.BlockSpec((1, 1, rs), lambda s, j, te, ns: (s, 0, 0), memory_space=pltpu.SMEM),
            pl.BlockSpec(memory_space=pl.ANY),
            pl.BlockSpec((1, 1, D, tf), lambda s, j, te, ns: (layer, te[s], 0, jmap(j, ns, s))),
            pl.BlockSpec((1, 1, D, tf), lambda s, j, te, ns: (layer, te[s], 0, jmap(j, ns, s))),
            pl.BlockSpec((1, 1, tf, D), lambda s, j, te, ns: (layer, te[s], jmap(j, ns, s), 0)),
        ],
        out_specs=pl.BlockSpec(memory_space=pl.ANY),
        scratch_shapes=[
            pltpu.VMEM((rs, D), BF16),
            pltpu.VMEM((3, sub, D), F32),
            pltpu.VMEM((rs, D), F32),
            pltpu.VMEM((D, tf), BF16),
            pltpu.VMEM((D, tf), BF16),
            pltpu.VMEM((tf, D), BF16),
            pltpu.SemaphoreType.DMA((3,)),
            pltpu.SemaphoreType.DMA(()),
        ],
    )
    return pl.pallas_call(
        _moe_kernel,
        out_shape=jax.ShapeDtypeStruct((2 * rows + sub, D), F32),
        grid_spec=grid_spec,
        compiler_params=_cparams(("arbitrary", "arbitrary")),
        name="moe_experts",
    )(tile_e, nsub, src, dst, h, wg, wu, wd)


POST_TM = 512


def _moe_post_kernel(x_ref, y0_ref, y1_ref, rw_ref, gpost_ref, gate_ref, o_ref):
    w = rw_ref[...]
    f = w[:, 0:1] * y0_ref[...] + w[:, 1:2] * y1_ref[...]
    o_ref[...] = x_ref[...] + gate_ref[0] * (_rms(f) * gpost_ref[...])


def _moe_post(x, y2, rw, gpost, mod, rows):
    tm = POST_TM
    off = rows // tm
    return pl.pallas_call(
        _moe_post_kernel,
        out_shape=jax.ShapeDtypeStruct((rows, D), F32),
        grid=(rows // tm,),
        in_specs=[
            pl.BlockSpec((tm, D), lambda i: (i, 0)),
            pl.BlockSpec((tm, D), lambda i: (i, 0)),
            pl.BlockSpec((tm, D), lambda i: (off + i, 0)),
            pl.BlockSpec((tm, LANES), lambda i: (i, 0)),
            pl.BlockSpec((1, D), lambda i: (0, 0)),
            _mod_spec(tm, 5),
        ],
        out_specs=pl.BlockSpec((tm, D), lambda i: (i, 0)),
        compiler_params=_cparams(("arbitrary",)),
        name="moe_post",
    )(x, y2, y2, rw, gpost.reshape(1, D), mod)


def _rope_tables():
    rows = SEQ // GRID_W
    row = jnp.repeat(jnp.arange(rows), GRID_W).astype(F32)
    col = jnp.tile(jnp.arange(GRID_W), rows).astype(F32)
    inv = ROPE_BASE ** (-jnp.arange(ROPE_PAIRS, dtype=F32) / ROPE_PAIRS)
    ang_r = row[:, None] * inv[None, :]
    ang_c = col[:, None] * inv[None, :]
    cr, sr, cc, sc = jnp.cos(ang_r), jnp.sin(ang_r), jnp.cos(ang_c), jnp.sin(ang_c)
    cos_t = jnp.concatenate([cr, cr, cc, cc], axis=-1)
    sin_t = jnp.concatenate([-sr, sr, -sc, sc], axis=-1)
    cos_t = jnp.concatenate([cos_t, jnp.ones((IN_TM, LANES), F32)], axis=0)
    sin_t = jnp.concatenate([sin_t, jnp.zeros((IN_TM, LANES), F32)], axis=0)
    return cos_t, sin_t


def kernel(x, c, ctx, c_ctx, w_mod, b_mod, g_attn_pre, g_attn_post, g_ffn_pre, g_ffn_post, w_in, w_out,
           sink_logit, lambda_q1, lambda_k1, lambda_q2, lambda_k2, subln_g, ffn_w_gate, ffn_w_up,
           ffn_w_down, moe_router, moe_w_gate, moe_w_up, moe_w_down):
    xs = jnp.concatenate([x.reshape(NL, D), ctx.reshape(NC, D)], axis=0)
    cin = jnp.concatenate([c, c_ctx[None, :], jnp.zeros((8 - B - 1, D), F32)], axis=0)
    mod_all = _modulation(cin, w_mod, b_mod)
    cos_t, sin_t = _rope_tables()
    w_in_bf = w_in.astype(BF16)

    for l in range(DEPTH):
        last = l == DEPTH - 1
        rows = NL if last else NR
        mod = mod_all[l].reshape(8, 1, N_MOD * D)
        lam_init = 0.8 - 0.6 * math.exp(-0.3 * l)
        lam_vecs = jnp.stack([lambda_q1[l], lambda_k1[l], lambda_q2[l], lambda_k2[l]])

        qkv = _in_proj(xs, g_attn_pre[l], mod, w_in_bf, l, cos_t, sin_t)
        attn_lat = _mixers(qkv, sink_logit[l], lam_vecs, subln_g[l], lam_init, lat=True)
        attn_ctx = None if last else _mixers(qkv, sink_logit[l], lam_vecs, subln_g[l], lam_init, lat=False)
        xs = _out_proj(xs, attn_lat, attn_ctx, w_out, l, g_attn_post[l], mod, rows)

        i = l // 2
        if l % 2 == 0:
            xs = _ffn_dense(xs, g_ffn_pre[l], g_ffn_post[l], mod, ffn_w_gate, ffn_w_up, ffn_w_down, i, rows)
        else:
            wr_pad = jnp.pad(moe_router[i], ((0, 0), (0, LANES - NE)))
            h, ri, rw = _router(xs, g_ffn_pre[l], mod, wr_pad, rows)
            plan, n_super = _moe_plan(ri, rows)

            def experts(n, plan=plan, h=h, i=i, rows=rows):
                return _moe_experts(h, tuple(p[:n] for p in plan), moe_w_gate, moe_w_up, moe_w_down, i, rows)

            y2 = lax.cond(n_super <= NE, functools.partial(experts, NE),
                          functools.partial(experts, plan[0].shape[0]))
            xs = _moe_post(xs, y2, rw, g_ffn_post[l], mod, rows)
    return xs.reshape(B, SEQ, D)
```

```python
import functools
import math

import jax
import jax.numpy as jnp
from jax import lax
from jax.experimental import pallas as pl
from jax.experimental.pallas import tpu as pltpu

D = 2048
B = 4
SEQ = 2048
DEPTH = 4
GRID_W = 64
CTX = 256
HD = 128
WINDOW = 128
A_HEADS = 8
A_KVH = 2
A_GROUP = 4
B_HEADS = 4
B_VDIM = 256
FF = 5632
NE = 8
N_MOD = 6
ROPE_BASE = 10000.0
ROPE_PAIRS = 32
ATTN_SCALE = HD ** -0.5
EPS = 1e-6
NEG = -1e30

NL = B * SEQ
NC = B * CTX
NR = NL + NC
IN_COLS = 4608
COL_QA, COL_QB, COL_KB, COL_VB, COL_KA, COL_VA = 0, 1024, 2048, 3072, 4096, 4352

LANES = 128
VMEM_LIMIT = 56 * 1024 * 1024

F32 = jnp.float32
BF16 = jnp.bfloat16


def _cparams(sem):
    return pltpu.CompilerParams(dimension_semantics=sem, vmem_limit_bytes=VMEM_LIMIT)


def _nt_dot(a, b):
    return lax.dot_general(a, b, (((1,), (1,)), ((), ())), preferred_element_type=F32)


def _rms(x):
    return x * lax.rsqrt(jnp.mean(x * x, axis=-1, keepdims=True) + EPS)


def _norm_mod(x, g, sh, sc):
    return (_rms(x) * g) * (1.0 + sc) + sh


def _mod_row(i, tm):
    return jnp.where(i < NL // tm, (i * tm) // SEQ, B)


def _mod_spec(tm, chunk, grid_rank=1):
    if grid_rank == 1:
        return pl.BlockSpec((1, 1, D), lambda i: (_mod_row(i, tm), 0, chunk))
    return pl.BlockSpec((1, 1, D), lambda i, j: (_mod_row(i, tm), 0, chunk))


MOD_TN = 1024


def _split_bf16(v):
    hi = v.astype(BF16)
    return hi, (v - hi.astype(F32)).astype(BF16)


def _mod_kernel(c_ref, w_ref, b_ref, o_ref):
    c = c_ref[...]
    a_hi, a_lo = _split_bf16(c * jax.nn.sigmoid(c))
    w_hi, w_lo = _split_bf16(w_ref[0])
    both = jnp.dot(jnp.concatenate([a_hi, a_lo], axis=0), w_hi, preferred_element_type=F32)
    o_ref[0] = both[0:8] + both[8:16] + jnp.dot(a_hi, w_lo, preferred_element_type=F32) + b_ref[0]


def _modulation(cin, w_mod, b_mod):
    return pl.pallas_call(
        _mod_kernel,
        out_shape=jax.ShapeDtypeStruct((DEPTH, 8, N_MOD * D), F32),
        grid=(DEPTH, N_MOD * D // MOD_TN),
        in_specs=[
            pl.BlockSpec((8, D), lambda l, n: (0, 0)),
            pl.BlockSpec((1, D, MOD_TN), lambda l, n: (l, 0, n)),
            pl.BlockSpec((1, 1, MOD_TN), lambda l, n: (l, 0, n)),
        ],
        out_specs=pl.BlockSpec((1, 8, MOD_TN), lambda l, n: (l, 0, n)),
        compiler_params=_cparams(("arbitrary", "arbitrary")),
        name="modulation",
    )(cin, w_mod, b_mod.reshape(DEPTH, 1, N_MOD * D))


IN_TM = 256
IN_CH = 512
LOG2E = math.log2(math.e)
Q_SCALE = ATTN_SCALE * LOG2E
IN_CHUNKS = ((0, 4, True), (1, 4, True), (8, 2, False), (2, 4, True), (3, 4, True),
             (4, 4, False), (5, 4, False), (6, 0, False), (7, 0, False))


def _inproj_kernel(x_ref, g_ref, sh_ref, sc_ref, cos_ref, sin_ref, w_ref, o_ref):
    hb = _norm_mod(x_ref[...], g_ref[...], sh_ref[0], sc_ref[0]).astype(BF16)
    cos = cos_ref[...]
    sin = sin_ref[...]
    lane = lax.broadcasted_iota(jnp.int32, (IN_TM, LANES), 1)
    first = (lane & 63) < 32
    for c, (dest, n_rope, is_q) in enumerate(IN_CHUNKS):
        acc = jnp.dot(hb, w_ref[0, :, c * IN_CH:(c + 1) * IN_CH], preferred_element_type=F32)
        for s in range(IN_CH // LANES):
            y = acc[:, s * LANES:(s + 1) * LANES]
            if s < n_rope:
                partner = jnp.where(first, pltpu.roll(y, 96, 1), pltpu.roll(y, 32, 1))
                y = y * cos + partner * sin
                if is_q:
                    y = y * Q_SCALE
            col = dest * IN_CH + s * LANES
            o_ref[:, col:col + LANES] = y.astype(BF16)


def _in_proj(x, g, mod, w_bf, layer, cos_t, sin_t):
    rows = x.shape[0]
    tm = IN_TM
    tab = lambda i: (jnp.where(i < NL // tm, i % (SEQ // tm), SEQ // tm), 0)
    return pl.pallas_call(
        _inproj_kernel,
        out_shape=jax.ShapeDtypeStruct((rows, IN_COLS), BF16),
        grid=(rows // tm,),
        in_specs=[
            pl.BlockSpec((tm, D), lambda i: (i, 0)),
            pl.BlockSpec((1, D), lambda i: (0, 0)),
            _mod_spec(tm, 0),
            _mod_spec(tm, 1),
            pl.BlockSpec((tm, LANES), tab),
            pl.BlockSpec((tm, LANES), tab),
            pl.BlockSpec((1, D, IN_COLS), lambda i: (layer, 0, 0)),
        ],
        out_specs=pl.BlockSpec((tm, IN_COLS), lambda i: (i, 0)),
        compiler_params=_cparams(("arbitrary",)),
        name="in_proj",
    )(x, g.reshape(1, D), mod, mod, cos_t, sin_t, w_bf)


ATT_TQ_LAT = 256
ATT_TQ_CTX = 256


def _attn_a_body(sink_ref, q_ref, kc_ref, vc_ref, kl_ref, vl_ref, o_ref, local, kv_heads):
    tq = q_ref.shape[0]
    a_win = tq + 2 * WINDOW
    i = pl.program_id(1)
    for kv in kv_heads:
        ksl = slice(kv * HD, (kv + 1) * HD)
        qs = jnp.concatenate(
            [q_ref[:, (kv * A_GROUP + g) * HD:(kv * A_GROUP + g + 1) * HD] for g in range(A_GROUP)], axis=0)
        sink = jnp.concatenate(
            [jnp.full((tq, 1), sink_ref[kv * A_GROUP + g] * LOG2E, F32) for g in range(A_GROUP)], axis=0)
        s_ctx = _nt_dot(qs, kc_ref[:, ksl])
        m = jnp.maximum(jnp.max(s_ctx, axis=-1, keepdims=True), sink)
        if local:
            st = pl.multiple_of(jnp.clip(i * tq - WINDOW, 0, SEQ - a_win), WINDOW)
            s_loc = _nt_dot(qs, kl_ref[pl.ds(st, a_win), ksl])
            qpos = i * tq + (lax.broadcasted_iota(jnp.int32, (A_GROUP * tq, a_win), 0) & (tq - 1))
            kpos = st + lax.broadcasted_iota(jnp.int32, (A_GROUP * tq, a_win), 1)
            s_loc = jnp.where(jnp.abs(qpos - kpos) <= WINDOW, s_loc, NEG)
            m = jnp.maximum(m, jnp.max(s_loc, axis=-1, keepdims=True))
        p_ctx = jnp.exp2(s_ctx - m)
        den = jnp.sum(p_ctx, axis=-1, keepdims=True) + jnp.exp2(sink - m)
        o = jnp.dot(p_ctx.astype(BF16), vc_ref[:, ksl], preferred_element_type=F32)
        if local:
            p_loc = jnp.exp2(s_loc - m)
            den = den + jnp.sum(p_loc, axis=-1, keepdims=True)
            o = o + jnp.dot(p_loc.astype(BF16), vl_ref[pl.ds(st, a_win), ksl], preferred_element_type=F32)
        o = o * (1.0 / den)
        for g in range(A_GROUP):
            h = kv * A_GROUP + g
            o_ref[:, h * HD:(h + 1) * HD] = o[g * tq:(g + 1) * tq].astype(BF16)


def _attn_b_body(lam_ref, g_ref, q_ref, kc_ref, vc_ref, kl_ref, vl_ref, o_ref, with_lat, lam_init, heads):
    lv = lam_ref[...]
    lam = (jnp.exp(jnp.sum(lv[0:1] * lv[1:2], axis=-1, keepdims=True))
           - jnp.exp(jnp.sum(lv[2:3] * lv[3:4], axis=-1, keepdims=True)) + lam_init)
    for h in heads:
        vsl = slice(h * B_VDIM, (h + 1) * B_VDIM)
        pv, rden, scores = [], [], []
        for mth in range(2):
            sl = slice((2 * h + mth) * HD, (2 * h + mth + 1) * HD)
            q = q_ref[:, sl]
            scores.append((_nt_dot(q, kc_ref[:, sl]), _nt_dot(q, kl_ref[:, sl]) if with_lat else None))
        for mth in range(2):
            s_c, s_l = scores[mth]
            mx = jnp.max(s_c, axis=-1, keepdims=True)
            if with_lat:
                mx = jnp.maximum(mx, jnp.max(s_l, axis=-1, keepdims=True))
            e_c = jnp.exp2(s_c - mx)
            den = jnp.sum(e_c, axis=-1, keepdims=True)
            o_m = jnp.dot(e_c.astype(BF16), vc_ref[:, vsl], preferred_element_type=F32)
            if with_lat:
                e_l = jnp.exp2(s_l - mx)
                den = den + jnp.sum(e_l, axis=-1, keepdims=True)
                o_m = o_m + jnp.dot(e_l.astype(BF16), vl_ref[:, vsl], preferred_element_type=F32)
            pv.append(o_m)
            rden.append(1.0 / den)
        o = pv[0] * rden[0] - pv[1] * (lam * rden[1])
        o = (_rms(o) * g_ref[...]) * (1.0 - lam_init)
        o_ref[:, vsl] = o.astype(BF16)


def _mixers_kernel(sink_ref, lam_ref, g_ref, qa_ref, kac_ref, vac_ref, qb_ref, kbc_ref, vbc_ref, *rest,
                   lat, lam_init):
    if lat:
        kal_ref, val_ref, kbl_ref, vbl_ref, oa_ref, ob_ref = rest
    else:
        oa_ref, ob_ref = rest
        kal_ref = val_ref = kbl_ref = vbl_ref = None
    mixer_a = functools.partial(_attn_a_body, sink_ref, qa_ref, kac_ref, vac_ref, kal_ref, val_ref, oa_ref, lat)
    mixer_b = functools.partial(_attn_b_body, lam_ref, g_ref, qb_ref, kbc_ref, vbc_ref, kbl_ref, vbl_ref, ob_ref,
                                lat, lam_init)
    mixer_a(range(A_KVH))
    mixer_b(range(B_HEADS))


def _mixers(qkv, sink, lam_vecs, subln_g, lam_init, lat):
    tq = ATT_TQ_LAT if lat else ATT_TQ_CTX
    nq = SEQ // tq if lat else CTX // tq
    row0 = 0 if lat else NL // tq
    ctx0 = NL // CTX
    qrow = lambda b, i: row0 + b * nq + i
    in_specs = [
        pl.BlockSpec(memory_space=pltpu.SMEM),
        pl.BlockSpec((4, HD), lambda b, i: (0, 0)),
        pl.BlockSpec((1, B_VDIM), lambda b, i: (0, 0)),
        pl.BlockSpec((tq, 1024), lambda b, i: (qrow(b, i), COL_QA // 1024)),
        pl.BlockSpec((CTX, 256), lambda b, i: (ctx0 + b, COL_KA // 256)),
        pl.BlockSpec((CTX, 256), lambda b, i: (ctx0 + b, COL_VA // 256)),
        pl.BlockSpec((tq, 1024), lambda b, i: (qrow(b, i), COL_QB // 1024)),
        pl.BlockSpec((CTX, 1024), lambda b, i: (ctx0 + b, COL_KB // 1024)),
        pl.BlockSpec((CTX, 1024), lambda b, i: (ctx0 + b, COL_VB // 1024)),
    ]
    args = [sink.reshape(A_HEADS), lam_vecs, subln_g.reshape(1, B_VDIM)] + [qkv] * 6
    if lat:
        in_specs += [
            pl.BlockSpec((SEQ, 256), lambda b, i: (b, COL_KA // 256)),
            pl.BlockSpec((SEQ, 256), lambda b, i: (b, COL_VA // 256)),
            pl.BlockSpec((SEQ, 1024), lambda b, i: (b, COL_KB // 1024)),
            pl.BlockSpec((SEQ, 1024), lambda b, i: (b, COL_VB // 1024)),
        ]
        args += [qkv] * 4
    rows = NL if lat else NC
    out = jax.ShapeDtypeStruct((rows, 1024), BF16)
    out_spec = pl.BlockSpec((tq, 1024), lambda b, i: (b * nq + i, 0))
    return pl.pallas_call(
        functools.partial(_mixers_kernel, lat=lat, lam_init=lam_init),
        out_shape=(out, out),
        grid=(B, nq),
        in_specs=in_specs,
        out_specs=(out_spec, out_spec),
        compiler_params=_cparams(("arbitrary", "arbitrary")),
        name="mixers_lat" if lat else "mixers_ctx",
    )(*args)


OUT_TM = 512


def _outproj_kernel(x_ref, w_ref, g_ref, gate_ref, oal_ref, obl_ref, *rest, has_ctx):
    if has_ctx:
        oac_ref, obc_ref, o_ref, wb_scr = rest
    else:
        o_ref, wb_scr = rest
    i = pl.program_id(0)

    @pl.when(i == 0)
    def _():
        wb_scr[...] = w_ref[0].astype(BF16)

    def project(oa_ref, ob_ref):
        y = (jnp.dot(oa_ref[...], wb_scr[0:1024, :], preferred_element_type=F32)
             + jnp.dot(ob_ref[...], wb_scr[1024:2048, :], preferred_element_type=F32))
        o_ref[...] = x_ref[...] + gate_ref[0] * (_rms(y) * g_ref[...])

    if has_ctx:
        n_lat = NL // OUT_TM

        @pl.when(i < n_lat)
        def _():
            project(oal_ref, obl_ref)

        @pl.when(i >= n_lat)
        def _():
            project(oac_ref, obc_ref)
    else:
        project(oal_ref, obl_ref)


def _out_proj(x, attn_lat, attn_ctx, w_out, layer, g, mod, rows):
    tm = OUT_TM
    n_lat = NL // tm
    has_ctx = attn_ctx is not None
    lat_spec = pl.BlockSpec((tm, 1024), lambda i: (jnp.minimum(i, n_lat - 1), 0))
    ctx_spec = pl.BlockSpec((tm, 1024), lambda i: (jnp.maximum(i - n_lat, 0), 0))
    in_specs = [
        pl.BlockSpec((tm, D), lambda i: (i, 0)),
        pl.BlockSpec((1, D, D), lambda i: (layer, 0, 0), pipeline_mode=pl.Buffered(1)),
        pl.BlockSpec((1, D), lambda i: (0, 0)),
        _mod_spec(tm, 2),
        lat_spec, lat_spec,
    ]
    args = [x, w_out, g.reshape(1, D), mod, *attn_lat]
    if has_ctx:
        in_specs += [ctx_spec, ctx_spec]
        args += list(attn_ctx)
    return pl.pallas_call(
        functools.partial(_outproj_kernel, has_ctx=has_ctx),
        out_shape=jax.ShapeDtypeStruct((rows, D), F32),
        grid=(rows // tm,),
        in_specs=in_specs,
        out_specs=pl.BlockSpec((tm, D), lambda i: (i, 0)),
        scratch_shapes=[pltpu.VMEM((D, D), BF16)],
        compiler_params=_cparams(("arbitrary",)),
        name="out_proj",
    )(*args)


FFN_TM = 1024
FFN_TF = 256
FFN_HALF = 512
FFN_NORM_ROWS = 16


def _swiglu_step(hb, wg, wu, wd):
    gt = jnp.dot(hb, wg, preferred_element_type=F32)
    up = jnp.dot(hb, wu, preferred_element_type=F32)
    act = (gt * jax.nn.sigmoid(gt) * up).astype(BF16)
    return jnp.dot(act, wd, preferred_element_type=F32)


def _ffn_kernel(x_ref, gpre_ref, sh_ref, sc_ref, gate_ref, gpost_ref, wg_ref, wu_ref, wd_ref, o_ref,
                h_scr, wgb, wub, wdb):
    j = pl.program_id(1)

    chunks = [slice(r, r + FFN_NORM_ROWS) for r in range(0, FFN_TM, FFN_NORM_ROWS)]

    @pl.when(j == 0)
    def _():
        for rows in chunks:
            h_scr[rows, :] = _norm_mod(x_ref[rows, :], gpre_ref[...], sh_ref[0], sc_ref[0]).astype(BF16)
            o_ref[rows, :] = jnp.zeros((FFN_NORM_ROWS, D), F32)

    wgb[...] = wg_ref[0].astype(BF16)
    wub[...] = wu_ref[0].astype(BF16)
    wdb[...] = wd_ref[0].astype(BF16)
    for r in range(FFN_TM // FFN_HALF):
        rows = slice(r * FFN_HALF, (r + 1) * FFN_HALF)
        o_ref[rows, :] += _swiglu_step(h_scr[rows, :], wgb[...], wub[...], wdb[...])

    @pl.when(j == pl.num_programs(1) - 1)
    def _():
        for rows in chunks:
            o_ref[rows, :] = x_ref[rows, :] + gate_ref[0] * (_rms(o_ref[rows, :]) * gpost_ref[...])


def _ffn_dense(x, gpre, gpost, mod, wg, wu, wd, layer, rows):
    tm, tf = FFN_TM, FFN_TF
    return pl.pallas_call(
        _ffn_kernel,
        out_shape=jax.ShapeDtypeStruct((rows, D), F32),
        grid=(rows // tm, FF // tf),
        in_specs=[
            pl.BlockSpec((tm, D), lambda i, j: (i, 0), pipeline_mode=pl.Buffered(1)),
            pl.BlockSpec((1, D), lambda i, j: (0, 0)),
            _mod_spec(tm, 3, 2),
            _mod_spec(tm, 4, 2),
            _mod_spec(tm, 5, 2),
            pl.BlockSpec((1, D), lambda i, j: (0, 0)),
            pl.BlockSpec((1, D, tf), lambda i, j: (layer, 0, j)),
            pl.BlockSpec((1, D, tf), lambda i, j: (layer, 0, j)),
            pl.BlockSpec((1, tf, D), lambda i, j: (layer, j, 0)),
        ],
        out_specs=pl.BlockSpec((tm, D), lambda i, j: (i, 0)),
        scratch_shapes=[pltpu.VMEM((tm, D), BF16), pltpu.VMEM((D, tf), BF16), pltpu.VMEM((D, tf), BF16),
                        pltpu.VMEM((tf, D), BF16)],
        compiler_params=_cparams(("arbitrary", "arbitrary")),
        name="ffn_dense",
    )(x, gpre.reshape(1, D), mod, mod, mod, gpost.reshape(1, D), wg, wu, wd)


RT_TM = 512
MOE_SUB = 256
MOE_RS = 2560
MOE_TF = 256


def _router_kernel(x_ref, gpre_ref, sh_ref, sc_ref, wr_ref, h_ref, ri_ref, rw_ref):
    h = _norm_mod(x_ref[...], gpre_ref[...], sh_ref[0], sc_ref[0])
    h_ref[...] = h
    h_hi, h_lo = _split_bf16(h)
    w_hi, w_lo = _split_bf16(wr_ref[...])
    logits = (jnp.dot(h_hi, w_hi, preferred_element_type=F32) + jnp.dot(h_lo, w_hi, preferred_element_type=F32)
              + jnp.dot(h_hi, w_lo, preferred_element_type=F32))
    lane = lax.broadcasted_iota(jnp.int32, logits.shape, 1)
    logits = jnp.where(lane < NE, logits, -jnp.inf)
    m1 = jnp.max(logits, axis=-1, keepdims=True)
    i1 = jnp.min(jnp.where(logits == m1, lane, LANES), axis=-1, keepdims=True)
    rest = jnp.where(lane == i1, -jnp.inf, logits)
    m2 = jnp.max(rest, axis=-1, keepdims=True)
    i2 = jnp.min(jnp.where(rest == m2, lane, LANES), axis=-1, keepdims=True)
    e2 = jnp.exp(m2 - m1)
    den = 1.0 + e2
    ri_ref[...] = jnp.where(lane == 0, i1, jnp.where(lane == 1, i2, 0))
    rw_ref[...] = jnp.where(lane == 0, 1.0 / den, jnp.where(lane == 1, e2 / den, 0.0))


def _router(x, gpre, mod, wr_pad, rows):
    tm = RT_TM
    return pl.pallas_call(
        _router_kernel,
        out_shape=(jax.ShapeDtypeStruct((rows, D), F32),
                   jax.ShapeDtypeStruct((rows, LANES), jnp.int32),
                   jax.ShapeDtypeStruct((rows, LANES), F32)),
        grid=(rows // tm,),
        in_specs=[
            pl.BlockSpec((tm, D), lambda i: (i, 0)),
            pl.BlockSpec((1, D), lambda i: (0, 0)),
            _mod_spec(tm, 3),
            _mod_spec(tm, 4),
            pl.BlockSpec((D, LANES), lambda i: (0, 0)),
        ],
        out_specs=(pl.BlockSpec((tm, D), lambda i: (i, 0)),
                   pl.BlockSpec((tm, LANES), lambda i: (i, 0)),
                   pl.BlockSpec((tm, LANES), lambda i: (i, 0))),
        compiler_params=_cparams(("arbitrary",)),
        name="moe_router",
    )(x, gpre.reshape(1, D), mod, mod, wr_pad)


def _moe_kernel(te_ref, nsub_ref, src_ref, dst_ref, h_hbm, wg_ref, wu_ref, wd_ref, y_hbm,
                x_scr, stage, acc_scr, wgb, wub, wdb, sem_in, sem_out):
    s = pl.program_id(0)
    j = pl.program_id(1)
    sub = MOE_SUB
    nsub = nsub_ref[s]
    active = nsub > 0

    def rows_of(c):
        return pl.ds(pl.multiple_of(c * sub, sub), sub)

    @pl.when(jnp.logical_and(s == 0, j == 0))
    def _():
        stage[0] = jnp.zeros((sub, D), F32)
        dump = pltpu.make_async_copy(stage.at[0], y_hbm.at[pl.ds(y_hbm.shape[0] - sub, sub)], sem_out)
        dump.start()
        dump.wait()

    @pl.when(jnp.logical_and(active, j == 0))
    def _():
        def issue(c, slot):
            def one(r, carry):
                pltpu.make_async_copy(h_hbm.at[pl.ds(src_ref[0, 0, c * sub + r], 1)],
                                      stage.at[slot, pl.ds(r, 1)], sem_in.at[slot]).start()
                return carry
            lax.fori_loop(0, sub, one, 0, unroll=8)

        issue(0, 0)

        @pl.when(nsub > 1)
        def _():
            issue(1, 1)

        def chunk(c, slot):
            @pl.when(c + 2 < nsub)
            def _():
                issue(c + 2, jnp.where(slot == 0, 2, slot - 1))

            pltpu.make_async_copy(h_hbm.at[pl.ds(0, sub)], stage.at[slot], sem_in.at[slot]).wait()
            x_scr[rows_of(c)] = stage[slot].astype(BF16)
            acc_scr[rows_of(c)] = jnp.zeros((sub, D), F32)
            return jnp.where(slot == 2, 0, slot + 1)
        lax.fori_loop(0, nsub, chunk, 0)

    @pl.when(active)
    def _():
        wgb[...] = wg_ref[0, 0].astype(BF16)
        wub[...] = wu_ref[0, 0].astype(BF16)
        wdb[...] = wd_ref[0, 0].astype(BF16)

        def block(start, n_chunks):
            rows = pl.ds(pl.multiple_of(start * sub, sub), n_chunks * sub)
            acc_scr[rows] += _swiglu_step(x_scr[rows], wgb[...], wub[...], wdb[...])

        def quad(q, carry):
            block(q * 4, 2)
            block(q * 4 + 2, 2)
            return carry
        lax.fori_loop(0, nsub >> 2, quad, 0)

        @pl.when((nsub & 2) == 2)
        def _():
            block((nsub >> 2) * 4, 2)

        @pl.when((nsub & 1) == 1)
        def _():
            block(nsub - 1, 1)

    @pl.when(jnp.logical_and(active, j == pl.num_programs(1) - 1))
    def _():
        def issue(g, carry):
            base = pl.multiple_of(g * 8, 8)
            for u in range(8):
                pltpu.make_async_copy(acc_scr.at[pl.ds(base + u, 1)],
                                      y_hbm.at[pl.ds(dst_ref[0, 0, base + u], 1)], sem_out).start()
            return carry
        lax.fori_loop(0, nsub * (sub // 8), issue, 0)

        def drain(c, carry):
            pltpu.make_async_copy(acc_scr.at[pl.ds(0, sub)], y_hbm.at[pl.ds(0, sub)], sem_out).wait()
            return carry
        lax.fori_loop(0, nsub, drain, 0)


def _moe_plan(ri, rows):
    sub, rs = MOE_SUB, MOE_RS
    n_asg = 2 * rows
    ns_max = (n_asg + NE * (sub - 1)) // rs + NE
    e_flat = jnp.concatenate([ri[:, 0], ri[:, 1]])
    _, order = lax.sort((e_flat, jnp.arange(n_asg, dtype=jnp.int32)), num_keys=1)
    counts = jnp.sum((e_flat[:, None] == jnp.arange(NE, dtype=jnp.int32)[None, :]).astype(jnp.int32), axis=0)
    ustart = jnp.cumsum(counts) - counts
    psz = ((counts + sub - 1) // sub) * sub
    nst = (psz + rs - 1) // rs
    st_end = jnp.cumsum(nst)
    total = st_end[NE - 1]
    s_idx = jnp.arange(ns_max, dtype=jnp.int32)
    valid = s_idx < total
    s_eff = jnp.minimum(s_idx, total - 1)
    e_s = jnp.sum((s_eff[:, None] >= st_end[None, :]).astype(jnp.int32), axis=1)
    k_s = s_eff - (st_end - nst)[e_s]
    nsub = jnp.where(valid, jnp.clip(psz[e_s] - k_s * rs, 0, rs) // sub, 0)
    lane = jnp.arange(rs, dtype=jnp.int32)[None, :]
    n_real = jnp.where(valid, jnp.clip(counts[e_s] - k_s * rs, 0, rs), 0)
    row_valid = lane < n_real[:, None]
    base = ustart[e_s] + k_s * rs
    order_pad = jnp.concatenate([order, jnp.zeros((rs,), jnp.int32)])
    asg = jnp.stack([lax.dynamic_slice(order_pad, (base[s],), (rs,)) for s in range(ns_max)])
    src = jnp.where(row_valid, asg % rows, 0)
    dst = jnp.where(row_valid, asg, n_asg + lane % sub)
    plan = (e_s.astype(jnp.int32), nsub.astype(jnp.int32),
            src.astype(jnp.int32).reshape(ns_max, 1, rs), dst.astype(jnp.int32).reshape(ns_max, 1, rs))
    return plan, total


def _moe_experts(h, plan, wg, wu, wd, layer, rows):
    sub, rs, tf = MOE_SUB, MOE_RS, MOE_TF
    tile_e, nsub, src, dst = plan
    ns_max = src.shape[0]
    nj = FF // tf
    jmap = lambda j, ns, s: jnp.where(ns[s] > 0, j, nj - 1)
    grid_spec = pltpu.PrefetchScalarGridSpec(
        num_scalar_prefetch=2,
        grid=(ns_max, nj),
        in_specs=[
            pl.BlockSpec((1, 1, rs), lambda s, j, te, ns: (s, 0, 0), memory_space=pltpu.SMEM),
            pl.BlockSpec((1, 1, rs), lambda s, j, te, ns: (s, 0, 0), memory_space=pltpu.SMEM),
            pl.BlockSpec(memory_space=pl.ANY),
            pl.BlockSpec((1, 1, D, tf), lambda s, j, te, ns: (layer, te[s], 0, jmap(j, ns, s))),
            pl.BlockSpec((1, 1, D, tf), lambda s, j, te, ns: (layer, te[s], 0, jmap(j, ns, s))),
            pl.BlockSpec((1, 1, tf, D), lambda s, j, te, ns: (layer, te[s], jmap(j, ns, s), 0)),
        ],
        out_specs=pl.BlockSpec(memory_space=pl.ANY),
        scratch_shapes=[
            pltpu.VMEM((rs, D), BF16),
            pltpu.VMEM((3, sub, D), F32),
            pltpu.VMEM((rs, D), F32),
            pltpu.VMEM((D, tf), BF16),
            pltpu.VMEM((D, tf), BF16),
            pltpu.VMEM((tf, D), BF16),
            pltpu.SemaphoreType.DMA((3,)),
            pltpu.SemaphoreType.DMA(()),
        ],
    )
    return pl.pallas_call(
        _moe_kernel,
        out_shape=jax.ShapeDtypeStruct((2 * rows + sub, D), F32),
        grid_spec=grid_spec,
        compiler_params=_cparams(("arbitrary", "arbitrary")),
        name="moe_experts",
    )(tile_e, nsub, src, dst, h, wg, wu, wd)


POST_TM = 512


def _moe_post_kernel(x_ref, y0_ref, y1_ref, rw_ref, gpost_ref, gate_ref, o_ref):
    w = rw_ref[...]
    f = w[:, 0:1] * y0_ref[...] + w[:, 1:2] * y1_ref[...]
    o_ref[...] = x_ref[...] + gate_ref[0] * (_rms(f) * gpost_ref[...])


def _moe_post(x, y2, rw, gpost, mod, rows):
    tm = POST_TM
    off = rows // tm
    return pl.pallas_call(
        _moe_post_kernel,
        out_shape=jax.ShapeDtypeStruct((rows, D), F32),
        grid=(rows // tm,),
        in_specs=[
            pl.BlockSpec((tm, D), lambda i: (i, 0)),
            pl.BlockSpec((tm, D), lambda i: (i, 0)),
            pl.BlockSpec((tm, D), lambda i: (off + i, 0)),
            pl.BlockSpec((tm, LANES), lambda i: (i, 0)),
            pl.BlockSpec((1, D), lambda i: (0, 0)),
            _mod_spec(tm, 5),
        ],
        out_specs=pl.BlockSpec((tm, D), lambda i: (i, 0)),
        compiler_params=_cparams(("arbitrary",)),
        name="moe_post",
    )(x, y2, y2, rw, gpost.reshape(1, D), mod)


def _rope_tables():
    rows = SEQ // GRID_W
    row = jnp.repeat(jnp.arange(rows), GRID_W).astype(F32)
    col = jnp.tile(jnp.arange(GRID_W), rows).astype(F32)
    inv = ROPE_BASE ** (-jnp.arange(ROPE_PAIRS, dtype=F32) / ROPE_PAIRS)
    ang_r = row[:, None] * inv[None, :]
    ang_c = col[:, None] * inv[None, :]
    cr, sr, cc, sc = jnp.cos(ang_r), jnp.sin(ang_r), jnp.cos(ang_c), jnp.sin(ang_c)
    cos_t = jnp.concatenate([cr, cr, cc, cc], axis=-1)
    sin_t = jnp.concatenate([-sr, sr, -sc, sc], axis=-1)
    cos_t = jnp.concatenate([cos_t, jnp.ones((IN_TM, LANES), F32)], axis=0)
    sin_t = jnp.concatenate([sin_t, jnp.zeros((IN_TM, LANES), F32)], axis=0)
    return cos_t, sin_t


def kernel(x, c, ctx, c_ctx, w_mod, b_mod, g_attn_pre, g_attn_post, g_ffn_pre, g_ffn_post, w_in, w_out,
           sink_logit, lambda_q1, lambda_k1, lambda_q2, lambda_k2, subln_g, ffn_w_gate, ffn_w_up,
           ffn_w_down, moe_router, moe_w_gate, moe_w_up, moe_w_down):
    xs = jnp.concatenate([x.reshape(NL, D), ctx.reshape(NC, D)], axis=0)
    cin = jnp.concatenate([c, c_ctx[None, :], jnp.zeros((8 - B - 1, D), F32)], axis=0)
    mod_all = _modulation(cin, w_mod, b_mod)
    cos_t, sin_t = _rope_tables()
    w_in_bf = w_in.astype(BF16)

    for l in range(DEPTH):
        last = l == DEPTH - 1
        rows = NL if last else NR
        mod = mod_all[l].reshape(8, 1, N_MOD * D)
        lam_init = 0.8 - 0.6 * math.exp(-0.3 * l)
        lam_vecs = jnp.stack([lambda_q1[l], lambda_k1[l], lambda_q2[l], lambda_k2[l]])

        qkv = _in_proj(xs, g_attn_pre[l], mod, w_in_bf, l, cos_t, sin_t)
        attn_lat = _mixers(qkv, sink_logit[l], lam_vecs, subln_g[l], lam_init, lat=True)
        attn_ctx = None if last else _mixers(qkv, sink_logit[l], lam_vecs, subln_g[l], lam_init, lat=False)
        xs = _out_proj(xs, attn_lat, attn_ctx, w_out, l, g_attn_post[l], mod, rows)

        i = l // 2
        if l % 2 == 0:
            xs = _ffn_dense(xs, g_ffn_pre[l], g_ffn_post[l], mod, ffn_w_gate, ffn_w_up, ffn_w_down, i, rows)
        else:
            wr_pad = jnp.pad(moe_router[i], ((0, 0), (0, LANES - NE)))
            h, ri, rw = _router(xs, g_ffn_pre[l], mod, wr_pad, rows)
            plan, n_super = _moe_plan(ri, rows)

            def experts(n, plan=plan, h=h, i=i, rows=rows):
                return _moe_experts(h, tuple(p[:n] for p in plan), moe_w_gate, moe_w_up, moe_w_down, i, rows)

            y2 = lax.cond(n_super <= NE, functools.partial(experts, NE),
                          functools.partial(experts, plan[0].shape[0]))
            xs = _moe_post(xs, y2, rw, g_ffn_post[l], mod, rows)
    return xs.reshape(B, SEQ, D)
```
